```python
import math
import jax, jax.numpy as jnp
from jax import lax
import numpy as np

D_MODEL = 2048
BATCH = 1
SEQ = 8192
DEPTH = 2

NUM_MIXERS = 2
N_GLA_LAYERS = (DEPTH + NUM_MIXERS - 1) // NUM_MIXERS
N_GDN_LAYERS = DEPTH // NUM_MIXERS

EPS = 1e-6
CHUNK = 64
PLE_DIM = 256
D_FF = ((8 * D_MODEL + 3 * 256 - 1) // (3 * 256)) * 256

GLA_HEADS = 4
GLA_DK = D_MODEL // (2 * GLA_HEADS)
GLA_DV = D_MODEL // GLA_HEADS
GLA_KEY_DIM = GLA_HEADS * GLA_DK
GLA_VAL_DIM = GLA_HEADS * GLA_DV
GLA_GATE_RANK = 16
GLA_GATE_NORM = 16.0
GLA_PROJ = 2 * GLA_KEY_DIM + 2 * GLA_VAL_DIM + 2 * GLA_GATE_RANK

GDN_QK_HEADS = D_MODEL // 128
GDN_V_HEADS = 2 * GDN_QK_HEADS
GDN_DK = 128
GDN_DV = 128
GDN_KEY_DIM = GDN_QK_HEADS * GDN_DK
GDN_VAL_DIM = GDN_V_HEADS * GDN_DV
GDN_CONV_DIM = 2 * GDN_KEY_DIM + GDN_VAL_DIM
CONV_W = 5
GDN_PROJ = GDN_CONV_DIM + GDN_VAL_DIM + 4 * GDN_V_HEADS

kernel_name = "hybrid_gla_gdn_encoder"


def rmsnorm(x, w):
    xf = x.astype(jnp.float32)
    y = xf * lax.rsqrt(jnp.mean(xf * xf, axis=-1, keepdims=True) + EPS)
    return (y * w.astype(jnp.float32)).astype(x.dtype)


def l2norm(x):
    xf = x.astype(jnp.float32)
    return xf * lax.rsqrt(jnp.sum(xf * xf, axis=-1, keepdims=True) + EPS)


def to_heads(x, n):
    b, t, _ = x.shape
    return x.reshape(b, t, n, -1).transpose(0, 2, 1, 3)


def from_heads(x):
    b, n, t, d = x.shape
    return x.transpose(0, 2, 1, 3).reshape(b, t, n * d)


def flip_t(x):
    return jnp.flip(x, axis=2)


def gla_chunked(q, k, v, g):
    b, h, t, dk = q.shape
    dv = v.shape[-1]
    nc = t // CHUNK
    q, k, g = (a.reshape(b, h, nc, CHUNK, dk) for a in (q, k, g))
    v = v.reshape(b, h, nc, CHUNK, dv)
    cum = jnp.cumsum(g, axis=-2)
    last = cum[..., -1:, :]
    q_dec = q * jnp.exp(cum)
    k_dec = k * jnp.exp(last - cum)
    chunk_decay = jnp.exp(last[..., 0, :])
    lower = jnp.tril(jnp.ones((CHUNK, CHUNK), dtype=bool))
    xs = tuple(jnp.moveaxis(a, 2, 0) for a in (q, k, v, cum, q_dec, k_dec, chunk_decay))

    def step(S, inp):
        q_i, k_i, v_i, c_i, qd_i, kd_i, cd_i = inp
        pair = jnp.where(lower[..., None],
                         jnp.exp(jnp.minimum(c_i[..., :, None, :] - c_i[..., None, :, :], 0.0)), 0.0)
        a_i = jnp.einsum('bhid,bhjd,bhijd->bhij', q_i, k_i, pair)
        o_i = jnp.einsum('bhij,bhjv->bhiv', a_i, v_i) + jnp.einsum('bhid,bhdv->bhiv', qd_i, S)
        S = S * cd_i[..., None] + jnp.einsum('bhcd,bhcv->bhdv', kd_i, v_i)
        return S, o_i

    S0 = jnp.zeros((b, h, dk, dv), jnp.float32)
    _, o = lax.scan(step, S0, xs)
    return jnp.moveaxis(o, 0, 2).reshape(b, h, t, dv)


def gated_delta_chunked(q, k, v, g, beta):
    b, h, t, dk = q.shape
    dv = v.shape[-1]
    nc = t // CHUNK
    q = q.reshape(b, h, nc, CHUNK, dk)
    k = k.reshape(b, h, nc, CHUNK, dk)
    v = v.reshape(b, h, nc, CHUNK, dv)
    cum = jnp.cumsum(g.reshape(b, h, nc, CHUNK), axis=-1)
    beta = beta.reshape(b, h, nc, CHUNK, 1)
    idx = jnp.arange(CHUNK)
    lower = idx[:, None] >= idx[None, :]
    strict = idx[:, None] > idx[None, :]
    decay = jnp.where(lower, jnp.exp(jnp.minimum(cum[..., :, None] - cum[..., None, :], 0.0)), 0.0)
    k_beta = k * beta
    tri = jnp.where(strict, jnp.einsum('bhnid,bhnjd->bhnij', k_beta, k) * decay, 0.0) \
        + jnp.eye(CHUNK, dtype=jnp.float32)
    rhs = jnp.concatenate([v * beta, k_beta * jnp.exp(cum)[..., None]], axis=-1)
    sol = lax.linalg.triangular_solve(tri, rhs, left_side=True, lower=True, unit_diagonal=True)
    u, w = sol[..., :dv], sol[..., dv:]
    attn = jnp.einsum('bhnid,bhnjd->bhnij', q, k) * decay
    last = cum[..., -1:]
    q_dec = q * jnp.exp(cum)[..., None]
    k_dec = k * jnp.exp(last - cum)[..., None]
    chunk_decay = jnp.exp(last)
    xs = tuple(jnp.moveaxis(a, 2, 0) for a in (u, w, attn, q_dec, k_dec, chunk_decay))

    def step(S, inp):
        u_i, w_i, a_i, qd_i, kd_i, cd_i = inp
        v_new = u_i - jnp.einsum('bhcd,bhdv->bhcv', w_i, S)
        o_i = jnp.einsum('bhcd,bhdv->bhcv', qd_i, S) + jnp.einsum('bhij,bhjv->bhiv', a_i, v_new)
        S = S * cd_i[..., None] + jnp.einsum('bhcd,bhcv->bhdv', kd_i, v_new)
        return S, o_i

    S0 = jnp.zeros((b, h, dk, dv), jnp.float32)
    _, o = lax.scan(step, S0, xs)
    return jnp.moveaxis(o, 0, 2).reshape(b, h, t, dv)


def centred_short_conv(x, w):
    c = x.shape[-1]
    return lax.conv_general_dilated(
        x, w[:, None, :].astype(x.dtype), window_strides=(1,),
        padding=[(CONV_W // 2, CONV_W // 2)],
        dimension_numbers=('NWC', 'WIO', 'NWC'), feature_group_count=c)


def gla_mixer(hn, w_in, w_gate_up, b_gate, out_norm, w_out):
    b, t, _ = hn.shape
    proj = jnp.einsum('btd,de->bte', hn, w_in)
    q, k, v, og, lr = jnp.split(
        proj, [GLA_KEY_DIM, 2 * GLA_KEY_DIM, 2 * GLA_KEY_DIM + GLA_VAL_DIM,
               2 * GLA_KEY_DIM + 2 * GLA_VAL_DIM], axis=-1)
    lr = lr.reshape(b, t, 2, GLA_GATE_RANK)
    gk = jnp.einsum('btzr,zrd->zbtd', lr, w_gate_up) + b_gate[:, None, None, :]
    logdec = jax.nn.log_sigmoid(gk.astype(jnp.float32)) / GLA_GATE_NORM
    qh = to_heads(q, GLA_HEADS).astype(jnp.float32) * (GLA_DK ** -0.5)
    kh = to_heads(k, GLA_HEADS).astype(jnp.float32)
    vh = to_heads(v, GLA_HEADS).astype(jnp.float32)
    g_fwd = to_heads(logdec[0], GLA_HEADS)
    g_bwd = to_heads(logdec[1], GLA_HEADS)
    o = gla_chunked(qh, kh, vh, g_fwd) \
        + flip_t(gla_chunked(flip_t(qh), flip_t(kh), flip_t(vh), flip_t(g_bwd)))
    o = rmsnorm(o, out_norm)
    o = from_heads(o).astype(hn.dtype) * jax.nn.silu(og)
    return jnp.einsum('bte,ed->btd', o, w_out)


def gdn_mixer(hn, w_in, conv_w, a_log, dt_bias, out_norm, w_out):
    b, t, _ = hn.shape
    proj = jnp.einsum('btd,de->bte', hn, w_in)
    qkv, z, ba = jnp.split(proj, [GDN_CONV_DIM, GDN_CONV_DIM + GDN_VAL_DIM], axis=-1)
    qkv = jax.nn.silu(centred_short_conv(qkv, conv_w))
    q, k, v = jnp.split(qkv, [GDN_KEY_DIM, 2 * GDN_KEY_DIM], axis=-1)
    rep = GDN_V_HEADS // GDN_QK_HEADS
    qh = jnp.repeat(l2norm(to_heads(q, GDN_QK_HEADS)), rep, axis=1) * (GDN_DK ** -0.5)
    kh = jnp.repeat(l2norm(to_heads(k, GDN_QK_HEADS)), rep, axis=1)
    vh = to_heads(v, GDN_V_HEADS).astype(jnp.float32)
    ba = ba.reshape(b, t, 2, 2, GDN_V_HEADS).astype(jnp.float32)
    beta = jax.nn.sigmoid(ba[:, :, 0]).transpose(2, 0, 3, 1)
    a_in = ba[:, :, 1].transpose(2, 0, 3, 1)
    g = -jnp.exp(a_log.astype(jnp.float32))[:, None, :, None] \
        * jax.nn.softplus(a_in + dt_bias.astype(jnp.float32)[:, None, :, None])
    o = gated_delta_chunked(qh, kh, vh, g[0], beta[0]) \
        + flip_t(gated_delta_chunked(flip_t(qh), flip_t(kh), flip_t(vh), flip_t(g[1]), flip_t(beta[1])))
    o = rmsnorm(o, out_norm)
    o = from_heads(o).astype(hn.dtype) * jax.nn.silu(z)
    return jnp.einsum('bte,ed->btd', o, w_out)


def swiglu(hn, w_in, w_out):
    gate, up = jnp.split(jnp.einsum('btd,de->bte', hn, w_in), 2, axis=-1)
    return jnp.einsum('btf,fd->btd', jax.nn.silu(gate) * up, w_out)


def setup_inputs(seed: int = 0) -> dict:
    key = jax.random.key(seed)
    ks = jax.random.split(key, 21)
    f32 = jnp.float32

    def nrm(k, shape, scale):
        return jax.random.normal(k, shape, f32) * scale

    def gain(k, shape):
        return 1.0 + 0.05 * jax.random.normal(k, shape, f32)

    x = nrm(ks[0], (BATCH, SEQ, D_MODEL), 1.0)
    p = nrm(ks[1], (DEPTH, BATCH, SEQ, PLE_DIM), 1.0)
    mixer_norm = gain(ks[2], (DEPTH, D_MODEL))
    gla_w_in = nrm(ks[3], (N_GLA_LAYERS, D_MODEL, GLA_PROJ), D_MODEL ** -0.5)
    gla_w_gate_up = nrm(ks[4], (N_GLA_LAYERS, 2, GLA_GATE_RANK, GLA_KEY_DIM), GLA_GATE_RANK ** -0.5)
    gla_b_gate = nrm(ks[5], (N_GLA_LAYERS, 2, GLA_KEY_DIM), 0.5)
    gla_out_norm = gain(ks[6], (N_GLA_LAYERS, GLA_DV))
    gla_w_out = nrm(ks[7], (N_GLA_LAYERS, GLA_VAL_DIM, D_MODEL), GLA_VAL_DIM ** -0.5)
    gdn_w_in = nrm(ks[8], (N_GDN_LAYERS, D_MODEL, GDN_PROJ), D_MODEL ** -0.5)
    gdn_conv = nrm(ks[9], (N_GDN_LAYERS, CONV_W, GDN_CONV_DIM), CONV_W ** -0.5)
    gdn_a_log = jnp.log(jax.random.uniform(ks[10], (N_GDN_LAYERS, 2, GDN_V_HEADS), f32, 1.0, 16.0))
    dt = jnp.exp(jax.random.uniform(ks[11], (N_GDN_LAYERS, 2, GDN_V_HEADS), f32,
                                    math.log(1e-3), math.log(1e-1)))
    gdn_dt_bias = dt + jnp.log(-jnp.expm1(-dt))
    gdn_out_norm = gain(ks[12], (N_GDN_LAYERS, GDN_DV))
    gdn_w_out = nrm(ks[13], (N_GDN_LAYERS, GDN_VAL_DIM, D_MODEL), GDN_VAL_DIM ** -0.5)
    ffn_norm = gain(ks[14], (DEPTH, D_MODEL))
    ffn_w_in = nrm(ks[15], (DEPTH, D_MODEL, 2 * D_FF), D_MODEL ** -0.5)
    ffn_w_out = nrm(ks[16], (DEPTH, D_FF, D_MODEL), D_FF ** -0.5)
    ple_norm = gain(ks[17], (DEPTH, D_MODEL))
    ple_w_gate = nrm(ks[18], (DEPTH, D_MODEL, D_MODEL), D_MODEL ** -0.5)
    ple_w_proj = nrm(ks[19], (DEPTH, PLE_DIM, D_MODEL), PLE_DIM ** -0.5)
    final_norm = gain(ks[20], (D_MODEL,))
    return {"x": x, "p": p, "mixer_norm": mixer_norm,
            "gla_w_in": gla_w_in, "gla_w_gate_up": gla_w_gate_up, "gla_b_gate": gla_b_gate,
            "gla_out_norm": gla_out_norm, "gla_w_out": gla_w_out,
            "gdn_w_in": gdn_w_in, "gdn_conv": gdn_conv, "gdn_a_log": gdn_a_log,
            "gdn_dt_bias": gdn_dt_bias, "gdn_out_norm": gdn_out_norm, "gdn_w_out": gdn_w_out,
            "ffn_norm": ffn_norm, "ffn_w_in": ffn_w_in, "ffn_w_out": ffn_w_out,
            "ple_norm": ple_norm, "ple_w_gate": ple_w_gate, "ple_w_proj": ple_w_proj,
            "final_norm": final_norm}


def reference(x, p, mixer_norm, gla_w_in, gla_w_gate_up, gla_b_gate, gla_out_norm, gla_w_out,
              gdn_w_in, gdn_conv, gdn_a_log, gdn_dt_bias, gdn_out_norm, gdn_w_out,
              ffn_norm, ffn_w_in, ffn_w_out, ple_norm, ple_w_gate, ple_w_proj, final_norm):
    h = x
    for i in range(DEPTH):
        j = i // NUM_MIXERS
        hn = rmsnorm(h, mixer_norm[i])
        if i % NUM_MIXERS == 0:
            mix = gla_mixer(hn, gla_w_in[j], gla_w_gate_up[j], gla_b_gate[j],
                            gla_out_norm[j], gla_w_out[j])
        else:
            mix = gdn_mixer(hn, gdn_w_in[j], gdn_conv[j], gdn_a_log[j], gdn_dt_bias[j],
                            gdn_out_norm[j], gdn_w_out[j])
        h = h + mix
        h = h + swiglu(rmsnorm(h, ffn_norm[i]), ffn_w_in[i], ffn_w_out[i])
        gate = jax.nn.sigmoid(jnp.einsum('btd,de->bte', rmsnorm(h, ple_norm[i]), ple_w_gate[i]))
        h = h + gate * jnp.einsum('btk,kd->btd', p[i], ple_w_proj[i])
    return rmsnorm(h, final_norm)
```

```python
import functools

import numpy as np
import jax
import jax.numpy as jnp
from jax import lax
from jax.experimental import pallas as pl
from jax.experimental.pallas import tpu as pltpu

F32 = jnp.float32
BF16 = jnp.bfloat16
EPS = 1e-6

V7X_LANES = 128
V7X_SUBLANES = 8
V7X_VMEM_LIMIT_BYTES = 56 * 1024 * 1024

GLA_HEADS = 4
GLA_DK = 256
GLA_DV = 512
GLA_RANK = 16
GLA_GATE_NORM = 16.0
GDN_QK_HEADS = 16
GDN_DK = 128
GDN_DV = 128
CONV_W = 5
SCAN_CHUNK = 128
DIAG = V7X_SUBLANES
PROJ_TN = 896


def _cparams(*sem):
    return pltpu.CompilerParams(dimension_semantics=sem,
                                vmem_limit_bytes=V7X_VMEM_LIMIT_BYTES)


def _dot(a, b):
    return jnp.dot(a, b, preferred_element_type=F32)


def _dot_nt(a, b):
    return lax.dot_general(a, b, (((1,), (1,)), ((), ())), preferred_element_type=F32)


def _dot_tn(a, b):
    return lax.dot_general(a, b, (((0,), (0,)), ((), ())), preferred_element_type=F32)


def _rms_scale(x):
    return lax.rsqrt(jnp.mean(x * x, axis=-1, keepdims=True) + EPS)


def _sigmoid(x):
    return 1.0 / (1.0 + jnp.exp(-x))


def _softplus(x):
    return jnp.maximum(x, 0.0) + jnp.log(1.0 + jnp.exp(-jnp.abs(x)))


def _split3(x):
    p1 = x.astype(BF16)
    r = x - p1.astype(F32)
    p2 = r.astype(BF16)
    p3 = (r - p2.astype(F32)).astype(BF16)
    return p1, p2, p3


def _norm_matmul_kernel(x_ref, nw_ref, w_ref, o_ref, xn_ref):
    @pl.when(pl.program_id(1) == 0)
    def _():
        x = x_ref[...]
        xn_ref[...] = (x * _rms_scale(x) * nw_ref[...]).astype(BF16)

    o_ref[...] = _dot(xn_ref[...], w_ref[...]).astype(o_ref.dtype)


def _norm_matmul(x, nw, w, *, tm, tn):
    t, d = x.shape
    n = w.shape[1]
    return pl.pallas_call(
        _norm_matmul_kernel,
        grid=(t // tm, n // tn),
        in_specs=[pl.BlockSpec((tm, d), lambda i, j: (i, 0)),
                  pl.BlockSpec((1, d), lambda i, j: (0, 0)),
                  pl.BlockSpec((d, tn), lambda i, j: (0, j))],
        out_specs=pl.BlockSpec((tm, tn), lambda i, j: (i, j)),
        out_shape=jax.ShapeDtypeStruct((t, n), F32),
        scratch_shapes=[pltpu.VMEM((tm, d), BF16)],
        compiler_params=_cparams("parallel", "arbitrary"),
        name="norm_matmul",
    )(x, nw.reshape(1, d), w)


def _matmul_res_kernel(a_ref, w_ref, h_ref, o_ref):
    o_ref[...] = h_ref[...] + _dot(a_ref[...], w_ref[...])


def _matmul_res(a, w, h, *, tm, tn):
    t, k = a.shape
    n = w.shape[1]
    return pl.pallas_call(
        _matmul_res_kernel,
        grid=(t // tm, n // tn),
        in_specs=[pl.BlockSpec((tm, k), lambda i, j: (i, 0)),
                  pl.BlockSpec((k, tn), lambda i, j: (0, j)),
                  pl.BlockSpec((tm, tn), lambda i, j: (i, j))],
        out_specs=pl.BlockSpec((tm, tn), lambda i, j: (i, j)),
        out_shape=jax.ShapeDtypeStruct((t, n), F32),
        compiler_params=_cparams("parallel", "arbitrary"),
        name="matmul_res",
    )(a, w, h)


def _ffn_kernel(x_ref, nw_ref, wg_ref, wu_ref, wo_ref, o_ref, xn_ref):
    @pl.when(pl.program_id(1) == 0)
    def _():
        x = x_ref[...]
        xn_ref[...] = (x * _rms_scale(x) * nw_ref[...]).astype(BF16)
        o_ref[...] = x

    xn = xn_ref[...]
    g = _dot(xn, wg_ref[...])
    u = _dot(xn, wu_ref[...])
    act = (g * _sigmoid(g) * u).astype(BF16)
    o_ref[...] += _dot(act, wo_ref[...])


def _ffn(x, nw, w_in, w_out, *, tm, tf):
    t, d = x.shape
    f = w_out.shape[0]
    nf = f // tf
    return pl.pallas_call(
        _ffn_kernel,
        grid=(t // tm, nf),
        in_specs=[pl.BlockSpec((tm, d), lambda i, j: (i, 0)),
                  pl.BlockSpec((1, d), lambda i, j: (0, 0)),
                  pl.BlockSpec((d, tf), lambda i, j: (0, j)),
                  pl.BlockSpec((d, tf), lambda i, j: (0, nf + j)),
                  pl.BlockSpec((tf, d), lambda i, j: (j, 0))],
        out_specs=pl.BlockSpec((tm, d), lambda i, j: (i, 0)),
        out_shape=jax.ShapeDtypeStruct((t, d), F32),
        scratch_shapes=[pltpu.VMEM((tm, d), BF16)],
        compiler_params=_cparams("parallel", "arbitrary"),
        name="ffn",
    )(x, nw.reshape(1, d), w_in, w_in, w_out)


def _ple_kernel(h_ref, nw_ref, wg_ref, p_ref, wp_ref, fn_ref, o_ref, *, final):
    h = h_ref[...]
    xn = (h * _rms_scale(h) * nw_ref[...]).astype(BF16)
    gate = _sigmoid(_dot(xn, wg_ref[...]))
    y = h + gate * _dot(p_ref[...].astype(BF16), wp_ref[...])
    if final:
        y = y * _rms_scale(y) * fn_ref[...]
    o_ref[...] = y


def _ple(h, nw, wg, p, wp, fn, *, tm, final):
    t, d = h.shape
    kp = p.shape[1]
    return pl.pallas_call(
        functools.partial(_ple_kernel, final=final),
        grid=(t // tm,),
        in_specs=[pl.BlockSpec((tm, d), lambda i: (i, 0)),
                  pl.BlockSpec((1, d), lambda i: (0, 0)),
                  pl.BlockSpec((d, d), lambda i: (0, 0)),
                  pl.BlockSpec((tm, kp), lambda i: (i, 0)),
                  pl.BlockSpec((kp, d), lambda i: (0, 0)),
                  pl.BlockSpec((1, d), lambda i: (0, 0))],
        out_specs=pl.BlockSpec((tm, d), lambda i: (i, 0)),
        out_shape=jax.ShapeDtypeStruct((t, d), F32),
        compiler_params=_cparams("parallel"),
        name="ple",
    )(h, nw.reshape(1, d), wg, p, wp, fn.reshape(1, d))


def _gla_levels(c):
    out, b = [], c // 2
    while b >= DIAG:
        out.append(b)
        b //= 2
    return out


def _gla_consts(c):
    idx = np.arange(c)
    i, m = idx[:, None], idx[None, :]
    blocks = [(m <= i), (m > i)]
    masks = []
    for b in _gla_levels(c):
        mid = (idx // (2 * b)) * (2 * b) + b - 1
        md = mid[:, None]
        blocks.append(np.where(i > md, (m > md) & (m <= i), (m > i) & (m <= md)))
        masks.append((i // (2 * b) == m // (2 * b)) & (i % (2 * b) >= b) & (m % (2 * b) < b))
    masks.append((i // DIAG == m // DIAG) & (m <= i))
    fwd_w = np.concatenate([bl.astype(np.float32) for bl in blocks], axis=0)
    bwd_w = np.concatenate([bl.astype(np.float32)[::-1, ::-1] for bl in blocks], axis=0)
    ones = np.ones((V7X_SUBLANES, c), np.float32)
    wcum = np.stack([np.concatenate([bwd_w, ones], 0), np.concatenate([fwd_w, ones], 0)])
    fwd_m = np.stack([mk.astype(np.float32) for mk in masks])
    bwd_m = np.stack([mk.astype(np.float32)[::-1, ::-1] for mk in masks])
    mask = np.stack([bwd_m, fwd_m])
    sel = (idx[None, None, :] % DIAG == np.arange(DIAG)[:, None, None])
    sel = np.broadcast_to(sel, (DIAG, GLA_DK, c)).astype(np.float32)
    return wcum, mask, sel


def _bcast_row_in_group(x, jj):
    c, n = x.shape
    x3 = x.reshape(c // DIAG, DIAG, n)
    return jnp.broadcast_to(x3[:, jj:jj + 1, :], (c // DIAG, DIAG, n)).reshape(c, n)


def _gla_kernel(q_ref, k_ref, v_ref, og_ref, lr_ref, wgu_ref, bg_ref, wcum_ref, mask_ref,
                sel_ref, onorm_ref, o_ref, st_ref, obwd_ref, *, c, nci, tb):
    d = pl.program_id(1)
    b = pl.program_id(2)
    nb = pl.num_programs(2)
    levels = _gla_levels(c)
    nlev = len(levels)

    @pl.when(b == 0)
    def _():
        st_ref[...] = jnp.zeros_like(st_ref)

    blk = jnp.where(d == 1, b, nb - 1 - b)

    def chunk(i, carry):
        ci = jnp.where(d == 1, i, nci - 1 - i)
        r0 = pl.multiple_of(ci * c, c)
        rows = pl.ds(r0, c)
        q = q_ref[rows, :] * (GLA_DK ** -0.5)
        k = k_ref[rows, :]
        vb = v_ref[rows, :].astype(BF16)

        gk = _dot(lr_ref[rows, :].astype(BF16), wgu_ref[...]) + bg_ref[...]
        g = (jnp.minimum(gk, 0.0) - jnp.log(1.0 + jnp.exp(-jnp.abs(gk)))) * (1.0 / GLA_GATE_NORM)
        gp = jnp.concatenate(_split3(g), axis=1)
        cums = _dot(wcum_ref[...], gp)
        cums = cums[:, :GLA_DK] + cums[:, GLA_DK:2 * GLA_DK] + cums[:, 2 * GLA_DK:]
        cum = cums[0:c]
        rest = cums[c:2 * c]
        tot = cums[(2 + nlev) * c:(2 + nlev) * c + 1]

        a = jnp.zeros((c, c), F32)
        for lv in range(nlev):
            e = jnp.exp(cums[(2 + lv) * c:(3 + lv) * c])
            a = a + _dot_nt((q * e).astype(BF16), (k * e).astype(BF16)) * mask_ref[lv]
        ad = jnp.zeros((c, c), F32)
        for jj in range(DIAG):
            kj = _bcast_row_in_group(k, jj)
            cj = _bcast_row_in_group(cum, jj)
            tj = q * kj * jnp.exp(jnp.minimum(cum - cj, 0.0))
            ad = ad + _dot(tj.astype(BF16), sel_ref[jj])
        a = a + ad * mask_ref[nlev]

        qd = (q * jnp.exp(cum)).astype(BF16)
        kd = (k * jnp.exp(rest)).astype(BF16)
        st = st_ref[...]
        o = _dot_nt(qd, st.astype(BF16)) + _dot(a.astype(BF16), vb)
        st_ref[...] = st * jnp.exp(tot) + _dot_tn(vb, kd)

        @pl.when(d == 0)
        def _():
            obwd_ref[pl.ds(pl.multiple_of(blk * tb + r0, c), c), :] = o

        @pl.when(d == 1)
        def _():
            ot = o + obwd_ref[pl.ds(pl.multiple_of(blk * tb + r0, c), c), :]
            y = ot * _rms_scale(ot) * onorm_ref[...]
            og = og_ref[rows, :]
            o_ref[rows, :] = (y * (og * _sigmoid(og))).astype(o_ref.dtype)

        return carry

    lax.fori_loop(0, nci, chunk, 0)


def _gla_scan(proj, wgu, bg, onorm, *, t):
    c = SCAN_CHUNK
    tb = min(t, 512)
    nb = t // tb
    nci = tb // c
    wcum, mask, sel = _gla_consts(c)
    nlev = len(_gla_levels(c))
    qoff = 0
    koff = GLA_HEADS * GLA_DK // GLA_DK
    voff = 2 * GLA_HEADS * GLA_DK // GLA_DV
    goff = voff + GLA_HEADS
    lroff = (2 * GLA_HEADS * GLA_DK + 2 * GLA_HEADS * GLA_DV) // V7X_LANES

    def tblk(d, b):
        return d * b + (1 - d) * (nb - 1 - b)

    return pl.pallas_call(
        functools.partial(_gla_kernel, c=c, nci=nci, tb=tb),
        grid=(GLA_HEADS, 2, nb),
        in_specs=[
            pl.BlockSpec((tb, GLA_DK), lambda h, d, b: (tblk(d, b), qoff + h)),
            pl.BlockSpec((tb, GLA_DK), lambda h, d, b: (tblk(d, b), koff + h)),
            pl.BlockSpec((tb, GLA_DV), lambda h, d, b: (tblk(d, b), voff + h)),
            pl.BlockSpec((tb, GLA_DV), lambda h, d, b: (tblk(d, b), goff + h)),
            pl.BlockSpec((tb, V7X_LANES), lambda h, d, b: (tblk(d, b), lroff)),
            pl.BlockSpec((None, None, V7X_LANES, GLA_DK), lambda h, d, b: (d, h, 0, 0)),
            pl.BlockSpec((None, None, 1, GLA_DK), lambda h, d, b: (d, h, 0, 0)),
            pl.BlockSpec((None, wcum.shape[1], c), lambda h, d, b: (d, 0, 0)),
            pl.BlockSpec((None, nlev + 1, c, c), lambda h, d, b: (d, 0, 0, 0)),
            pl.BlockSpec((DIAG, GLA_DK, c), lambda h, d, b: (0, 0, 0)),
            pl.BlockSpec((1, GLA_DV), lambda h, d, b: (0, 0)),
        ],
        out_specs=pl.BlockSpec((tb, GLA_DV), lambda h, d, b: (d * b, h)),
        out_shape=jax.ShapeDtypeStruct((t, GLA_HEADS * GLA_DV), BF16),
        scratch_shapes=[pltpu.VMEM((GLA_DV, GLA_DK), F32),
                        pltpu.VMEM((t, GLA_DV), F32)],
        compiler_params=_cparams("arbitrary", "arbitrary", "arbitrary"),
        name="gla_scan",
    )(proj, proj, proj, proj, proj, wgu, bg,
      jnp.asarray(wcum, BF16), jnp.asarray(mask, F32), jnp.asarray(sel, BF16),
      onorm.reshape(1, GLA_DV))


def _gla_gate_weights(w_gate_up, b_gate):
    w = w_gate_up.reshape(2, GLA_RANK, GLA_HEADS, GLA_DK).transpose(0, 2, 1, 3)
    z = jnp.zeros((GLA_HEADS, V7X_LANES, GLA_DK), F32)
    fwd = z.at[:, 0:GLA_RANK].set(w[0])
    bwd = z.at[:, GLA_RANK:2 * GLA_RANK].set(w[1])
    wgu = jnp.stack([bwd, fwd]).astype(BF16)
    bg = jnp.stack([b_gate[1], b_gate[0]]).reshape(2, GLA_HEADS, 1, GLA_DK)
    return wgu, bg


def _gdn_prep_kernel(prev_ref, cur_ref, next_ref, cw_ref, o_ref, xp_ref, *, tb, l2, nq):
    i = pl.program_id(0)
    j = pl.program_id(1)
    halo = V7X_SUBLANES
    xp_ref[0:halo, :] = jnp.where(i > 0, prev_ref[...], 0.0)
    xp_ref[halo:halo + tb, :] = cur_ref[...]
    xp_ref[halo + tb:2 * halo + tb, :] = jnp.where(i < pl.num_programs(0) - 1, next_ref[...], 0.0)
    acc = jnp.zeros(cur_ref.shape, F32)
    for w in range(CONV_W):
        acc = acc + xp_ref[pl.ds(halo - CONV_W // 2 + w, tb), :] * cw_ref[w:w + 1, :]
    y = acc * _sigmoid(acc)
    if l2:
        qs = jnp.where(j < nq, GDN_DK ** -0.5, 1.0)
        for hh in range(cur_ref.shape[1] // GDN_DK):
            seg = y[:, hh * GDN_DK:(hh + 1) * GDN_DK]
            ss = jnp.sum(seg * seg, axis=-1, keepdims=True)
            o_ref[:, hh * GDN_DK:(hh + 1) * GDN_DK] = seg * (lax.rsqrt(ss + EPS) * qs)
    else:
        o_ref[...] = y


def _gdn_prep(proj, conv_w, *, t, col0, ncols, l2):
    tb = min(t, 512)
    cb = 512
    hb = tb // V7X_SUBLANES
    nt = t // tb
    c0 = col0 // cb
    nq = GDN_QK_HEADS * GDN_DK // cb
    return pl.pallas_call(
        functools.partial(_gdn_prep_kernel, tb=tb, l2=l2, nq=nq),
        grid=(nt, ncols // cb),
        in_specs=[
            pl.BlockSpec((V7X_SUBLANES, cb), lambda i, j: (jnp.maximum(i * hb - 1, 0), c0 + j)),
            pl.BlockSpec((tb, cb), lambda i, j: (i, c0 + j)),
            pl.BlockSpec((V7X_SUBLANES, cb), lambda i, j: (jnp.minimum((i + 1) * hb, nt * hb - 1), c0 + j)),
            pl.BlockSpec((CONV_W, cb), lambda i, j: (0, c0 + j)),
        ],
        out_specs=pl.BlockSpec((tb, cb), lambda i, j: (i, j)),
        out_shape=jax.ShapeDtypeStruct((t, ncols), F32),
        scratch_shapes=[pltpu.VMEM((tb + 2 * V7X_SUBLANES, cb), F32)],
        compiler_params=_cparams("parallel", "parallel"),
        name="gdn_prep",
    )(proj, proj, proj, conv_w)


def _gdn_consts(c):
    idx = np.arange(c)
    i, m = idx[:, None], idx[None, :]
    u_f = (i <= m)
    uv_f = np.concatenate([u_f, ~u_f], axis=1).astype(np.float32)
    u_b = (i >= m)
    uv_b = np.concatenate([u_b, ~u_b], axis=1).astype(np.float32)
    uv = np.stack([uv_b, uv_f])

    def pair(b):
        return (i // (2 * b) == m // (2 * b)) & (i // b != m // b)

    lower, strict = (m <= i), (m < i)
    blk = (i // 16 == m // 16)
    levels = []
    b = 16
    while 2 * b <= c:
        levels.append(pair(b))
        b *= 2
    fwd = [lower, strict, blk & strict] + [p & strict for p in levels]
    mask_f = np.stack([x.astype(np.float32) for x in fwd])
    mask_b = np.stack([x.astype(np.float32)[::-1, ::-1] for x in fwd])
    mask = np.stack([mask_b, mask_f])
    return uv, mask, np.eye(c, dtype=np.float32), len(levels)


def _gdn_kernel(q_ref, k_ref, v_ref, z_ref, gt_ref, alog_ref, dtb_ref, uv_ref, mask_ref,
                eye_ref, onorm_ref, o_ref, s_ref, obwd_ref, *, c, nci, tb, nlev):
    d = pl.program_id(1)
    b = pl.program_id(2)
    nb = pl.num_programs(2)

    @pl.when(b == 0)
    def _():
        s_ref[...] = jnp.zeros_like(s_ref)

    blk = jnp.where(d == 1, b, nb - 1 - b)
    row8 = lax.broadcasted_iota(jnp.int32, (V7X_SUBLANES, c), 0)

    def chunk(i, carry):
        ci = jnp.where(d == 1, i, nci - 1 - i)
        r0 = pl.multiple_of(ci * c, c)
        rows = pl.ds(r0, c)
        gt = gt_ref[ci]
        beta8 = _sigmoid(gt)
        g8 = -jnp.exp(alog_ref[...]) * _softplus(gt + dtb_ref[...])
        gp = jnp.concatenate(_split3(g8), axis=0)
        cl = _dot(gp, uv_ref[...])
        cl = cl[0:8] + cl[8:16] + cl[16:24]
        cum8 = cl[:, :c]
        rest8 = cl[:, c:]
        x8 = jnp.where(row8 < 2, beta8, cum8)
        cols = jnp.concatenate([x8, jnp.zeros((c - V7X_SUBLANES, c), F32)], axis=0).T

        qf = q_ref[rows, :]
        kf = k_ref[rows, :]
        qb = qf.astype(BF16)
        kb = kf.astype(BF16)
        kk = _dot_nt(kb, kb)
        qk = _dot_nt(qb, kb)
        kt = kf.T
        lower = mask_ref[0]
        strict = mask_ref[1]
        eye = eye_ref[...]

        for hh in range(2):
            beta_c = cols[:, hh:hh + 1]
            cum_c = cols[:, 2 + hh:3 + hh]
            cum_r = cum8[2 + hh:3 + hh, :]
            rest_r = rest8[2 + hh:3 + hh, :]
            dec = jnp.exp(jnp.minimum(cum_c - cum_r, 0.0))
            lm = kk * dec * beta_c * strict
            at = qk * dec * lower

            n1 = -(lm * mask_ref[2])
            x = eye + n1
            p = n1.astype(BF16)
            for _ in range(3):
                p32 = _dot(p, p)
                p = p32.astype(BF16)
                x = x + _dot(x.astype(BF16), p)
            for lv in range(nlev):
                cb = (lm * mask_ref[3 + lv]).astype(BF16)
                xb = x.astype(BF16)
                x = x - _dot(xb, _dot(cb, xb).astype(BF16))

            e_c = jnp.exp(cum_c)
            vh = v_ref[rows, hh * GDN_DV:(hh + 1) * GDN_DV]
            rhs = jnp.concatenate([vh * beta_c, kf * (beta_c * e_c)], axis=1).astype(BF16)
            uw = _dot(x.astype(BF16), rhs)
            u = uw[:, :GDN_DV]
            w = uw[:, GDN_DV:]
            qd = (qf * e_c).astype(BF16)
            kdt = (kt * jnp.exp(rest_r)).astype(BF16)

            s = s_ref[hh]
            r1 = _dot(jnp.concatenate([w.astype(BF16), qd], axis=0), s.astype(BF16))
            v_new = u - r1[:c]
            r2 = _dot(jnp.concatenate([at.astype(BF16), kdt], axis=0), v_new.astype(BF16))
            o = r1[c:] + r2[:c]
            tot_r = cum_r + rest_r
            s_ref[hh] = s * jnp.exp(tot_r) + r2[c:]

            orow = pl.ds(pl.multiple_of(blk * tb + r0, c), c)
            ocol = slice(hh * GDN_DV, (hh + 1) * GDN_DV)

            @pl.when(d == 0)
            def _():
                obwd_ref[orow, ocol] = o

            @pl.when(d == 1)
            def _():
                ot = o + obwd_ref[orow, ocol]
                y = ot * _rms_scale(ot) * onorm_ref[...]
                z = z_ref[rows, ocol]
                o_ref[rows, ocol] = (y * (z * _sigmoid(z))).astype(o_ref.dtype)

        return carry

    lax.fori_loop(0, nci, chunk, 0)


def _gdn_scan(qkn, vv, proj, gt, alog, dtb, onorm, *, t):
    c = SCAN_CHUNK
    tb = min(t, 512)
    nb = t // tb
    nci = tb // c
    uv, mask, eye, nlev = _gdn_consts(c)
    hw = 2 * GDN_DV
    zoff = (2 * GDN_QK_HEADS * GDN_DK + 2 * GDN_QK_HEADS * GDN_DV) // hw

    def tblk(d, b):
        return d * b + (1 - d) * (nb - 1 - b)

    return pl.pallas_call(
        functools.partial(_gdn_kernel, c=c, nci=nci, tb=tb, nlev=nlev),
        grid=(GDN_QK_HEADS, 2, nb),
        in_specs=[
            pl.BlockSpec((tb, GDN_DK), lambda h, d, b: (tblk(d, b), h)),
            pl.BlockSpec((tb, GDN_DK), lambda h, d, b: (tblk(d, b), GDN_QK_HEADS + h)),
            pl.BlockSpec((tb, hw), lambda h, d, b: (tblk(d, b), h)),
            pl.BlockSpec((tb, hw), lambda h, d, b: (tblk(d, b), zoff + h)),
            pl.BlockSpec((None, None, nci, V7X_SUBLANES, c), lambda h, d, b: (d, h, tblk(d, b), 0, 0)),
            pl.BlockSpec((None, None, V7X_SUBLANES, c), lambda h, d, b: (d, h, 0, 0)),
            pl.BlockSpec((None, None, V7X_SUBLANES, c), lambda h, d, b: (d, h, 0, 0)),
            pl.BlockSpec((None, c, 2 * c), lambda h, d, b: (d, 0, 0)),
            pl.BlockSpec((None, mask.shape[1], c, c), lambda h, d, b: (d, 0, 0, 0)),
            pl.BlockSpec((c, c), lambda h, d, b: (0, 0)),
            pl.BlockSpec((1, GDN_DV), lambda h, d, b: (0, 0)),
        ],
        out_specs=pl.BlockSpec((tb, hw), lambda h, d, b: (d * b, h)),
        out_shape=jax.ShapeDtypeStruct((t, 2 * GDN_QK_HEADS * GDN_DV), BF16),
        scratch_shapes=[pltpu.VMEM((2, GDN_DK, GDN_DV), F32),
                        pltpu.VMEM((t, hw), F32)],
        compiler_params=_cparams("arbitrary", "arbitrary", "arbitrary"),
        name="gdn_scan",
    )(qkn, qkn, vv, proj, gt, alog, dtb,
      jnp.asarray(uv, BF16), jnp.asarray(mask, F32), jnp.asarray(eye, F32),
      onorm.reshape(1, GDN_DV))


def _gdn_gate_layout(ba, a_log, dt_bias, *, t):
    c = SCAN_CHUNK
    g = ba.reshape(t, 2, 2, GDN_QK_HEADS, 2)
    g = g.transpose(2, 3, 1, 4, 0).reshape(2, GDN_QK_HEADS, 4, t)
    g = jnp.concatenate([g, jnp.zeros_like(g)], axis=2)
    g = g.reshape(2, GDN_QK_HEADS, V7X_SUBLANES, t // c, c).transpose(0, 1, 3, 2, 4)
    g = jnp.stack([g[1], g[0]])

    def rows(pv):
        pv = pv.reshape(2, GDN_QK_HEADS, 2)
        z = jnp.zeros_like(pv)
        r = jnp.concatenate([z, pv, z, z], axis=2)
        r = jnp.broadcast_to(r[..., None], (2, GDN_QK_HEADS, V7X_SUBLANES, c))
        return jnp.stack([r[1], r[0]])

    return g, rows(a_log.astype(F32)), rows(dt_bias.astype(F32))


def _pad_cols(w, n):
    return jnp.concatenate([w, jnp.zeros((w.shape[0], n - w.shape[1]), w.dtype)], axis=1)


def kernel(x, p, mixer_norm, gla_w_in, gla_w_gate_up, gla_b_gate, gla_out_norm, gla_w_out,
           gdn_w_in, gdn_conv, gdn_a_log, gdn_dt_bias, gdn_out_norm, gdn_w_out,
           ffn_norm, ffn_w_in, ffn_w_out, ple_norm, ple_w_gate, ple_w_proj, final_norm):
    _, t, dm = x.shape
    depth = mixer_norm.shape[0]
    h = x.reshape(t, dm)
    tm = min(t, 1024)
    for i in range(depth):
        j = i // 2
        if i % 2 == 0:
            n_pad = -(-gla_w_in.shape[2] // PROJ_TN) * PROJ_TN
            w_in = _pad_cols(gla_w_in[j].astype(BF16), n_pad)
            proj = _norm_matmul(h, mixer_norm[i], w_in, tm=tm, tn=PROJ_TN)
            wgu, bg = _gla_gate_weights(gla_w_gate_up[j], gla_b_gate[j])
            o = _gla_scan(proj, wgu, bg, gla_out_norm[j], t=t)
            h = _matmul_res(o, gla_w_out[j].astype(BF16), h, tm=tm, tn=512)
        else:
            n_pad = -(-gdn_w_in.shape[2] // PROJ_TN) * PROJ_TN
            w_in = _pad_cols(gdn_w_in[j].astype(BF16), n_pad)
            proj = _norm_matmul(h, mixer_norm[i], w_in, tm=tm, tn=PROJ_TN)
            kd = GDN_QK_HEADS * GDN_DK
            vd = 2 * GDN_QK_HEADS * GDN_DV
            qkn = _gdn_prep(proj, gdn_conv[j], t=t, col0=0, ncols=2 * kd, l2=True)
            vv = _gdn_prep(proj, gdn_conv[j], t=t, col0=2 * kd, ncols=vd, l2=False)
            ba = proj[:, 2 * kd + 2 * vd:2 * kd + 2 * vd + 4 * 2 * GDN_QK_HEADS]
            gt, alog, dtb = _gdn_gate_layout(ba, gdn_a_log[j], gdn_dt_bias[j], t=t)
            o = _gdn_scan(qkn, vv, proj, gt, alog, dtb, gdn_out_norm[j], t=t)
            h = _matmul_res(o, gdn_w_out[j].astype(BF16), h, tm=tm, tn=512)
        h = _ffn(h, ffn_norm[i], ffn_w_in[i].astype(BF16), ffn_w_out[i].astype(BF16),
                 tm=min(t, 512), tf=512)
        h = _ple(h, ple_norm[i], ple_w_gate[i].astype(BF16), p[i].reshape(t, -1),
                 ple_w_proj[i].astype(BF16), final_norm, tm=min(t, 256), final=(i == depth - 1))
    return h.reshape(x.shape)
```

```python
import functools

import numpy as np
import jax
import jax.numpy as jnp
from jax import lax
from jax.experimental import pallas as pl
from jax.experimental.pallas import tpu as pltpu

F32 = jnp.float32
BF16 = jnp.bfloat16
EPS = 1e-6

V7X_LANES = 128
V7X_SUBLANES = 8
V7X_VMEM_LIMIT_BYTES = 56 * 1024 * 1024

GLA_HEADS = 4
GLA_DK = 256
GLA_DV = 512
GLA_RANK = 16
GLA_GATE_NORM = 16.0
GDN_QK_HEADS = 16
GDN_DK = 128
GDN_DV = 128
CONV_W = 5
SCAN_CHUNK = 128
DIAG = V7X_SUBLANES
PROJ_TN = 896


def _cparams(*sem):
    return pltpu.CompilerParams(dimension_semantics=sem,
                                vmem_limit_bytes=V7X_VMEM_LIMIT_BYTES)


def _dot(a, b):
    return jnp.dot(a, b, preferred_element_type=F32)


def _dot_nt(a, b):
    return lax.dot_general(a, b, (((1,), (1,)), ((), ())), preferred_element_type=F32)


def _dot_tn(a, b):
    return lax.dot_general(a, b, (((0,), (0,)), ((), ())), preferred_element_type=F32)


def _rms_scale(x):
    return lax.rsqrt(jnp.mean(x * x, axis=-1, keepdims=True) + EPS)


def _sigmoid(x):
    return 1.0 / (1.0 + jnp.exp(-x))


def _softplus(x):
    return jnp.maximum(x, 0.0) + jnp.log(1.0 + jnp.exp(-jnp.abs(x)))


def _split3(x):
    p1 = x.astype(BF16)
    r = x - p1.astype(F32)
    p2 = r.astype(BF16)
    p3 = (r - p2.astype(F32)).astype(BF16)
    return p1, p2, p3


def _norm_matmul_kernel(x_ref, nw_ref, w_ref, o_ref, xn_ref):
    @pl.when(pl.program_id(1) == 0)
    def _():
        x = x_ref[...]
        xn_ref[...] = (x * _rms_scale(x) * nw_ref[...]).astype(BF16)

    o_ref[...] = _dot(xn_ref[...], w_ref[...]).astype(o_ref.dtype)


def _norm_matmul(x, nw, w, *, tm, tn):
    t, d = x.shape
    n = w.shape[1]
    return pl.pallas_call(
        _norm_matmul_kernel,
        grid=(t // tm, n // tn),
        in_specs=[pl.BlockSpec((tm, d), lambda i, j: (i, 0)),
                  pl.BlockSpec((1, d), lambda i, j: (0, 0)),
                  pl.BlockSpec((d, tn), lambda i, j: (0, j))],
        out_specs=pl.BlockSpec((tm, tn), lambda i, j: (i, j)),
        out_shape=jax.ShapeDtypeStruct((t, n), F32),
        scratch_shapes=[pltpu.VMEM((tm, d), BF16)],
        compiler_params=_cparams("parallel", "arbitrary"),
        name="norm_matmul",
    )(x, nw.reshape(1, d), w)


def _matmul_res_kernel(a_ref, w_ref, h_ref, o_ref):
    o_ref[...] = h_ref[...] + _dot(a_ref[...], w_ref[...])


def _matmul_res(a, w, h, *, tm, tn):
    t, k = a.shape
    n = w.shape[1]
    return pl.pallas_call(
        _matmul_res_kernel,
        grid=(t // tm, n // tn),
        in_specs=[pl.BlockSpec((tm, k), lambda i, j: (i, 0)),
                  pl.BlockSpec((k, tn), lambda i, j: (0, j)),
                  pl.BlockSpec((tm, tn), lambda i, j: (i, j))],
        out_specs=pl.BlockSpec((tm, tn), lambda i, j: (i, j)),
        out_shape=jax.ShapeDtypeStruct((t, n), F32),
        compiler_params=_cparams("parallel", "arbitrary"),
        name="matmul_res",
    )(a, w, h)


def _ffn_kernel(x_ref, nw_ref, wg_ref, wu_ref, wo_ref, o_ref, xn_ref):
    @pl.when(pl.program_id(1) == 0)
    def _():
        x = x_ref[...]
        xn_ref[...] = (x * _rms_scale(x) * nw_ref[...]).astype(BF16)
        o_ref[...] = x

    xn = xn_ref[...]
    g = _dot(xn, wg_ref[...])
    u = _dot(xn, wu_ref[...])
    act = (g * _sigmoid(g) * u).astype(BF16)
    o_ref[...] += _dot(act, wo_ref[...])


def _ffn(x, nw, w_in, w_out, *, tm, tf):
    t, d = x.shape
    f = w_out.shape[0]
    nf = f // tf
    return pl.pallas_call(
        _ffn_kernel,
        grid=(t // tm, nf),
        in_specs=[pl.BlockSpec((tm, d), lambda i, j: (i, 0)),
                  pl.BlockSpec((1, d), lambda i, j: (0, 0)),
                  pl.BlockSpec((d, tf), lambda i, j: (0, j)),
                  pl.BlockSpec((d, tf), lambda i, j: (0, nf + j)),
                  pl.BlockSpec((tf, d), lambda i, j: (j, 0))],
        out_specs=pl.BlockSpec((tm, d), lambda i, j: (i, 0)),
        out_shape=jax.ShapeDtypeStruct((t, d), F32),
        scratch_shapes=[pltpu.VMEM((tm, d), BF16)],
        compiler_params=_cparams("parallel", "arbitrary"),
        name="ffn",
    )(x, nw.reshape(1, d), w_in, w_in, w_out)


def _ple_kernel(h_ref, nw_ref, wg_ref, p_ref, wp_ref, fn_ref, o_ref, *, final):
    h = h_ref[...]
    xn = (h * _rms_scale(h) * nw_ref[...]).astype(BF16)
    gate = _sigmoid(_dot(xn, wg_ref[...]))
    y = h + gate * _dot(p_ref[...].astype(BF16), wp_ref[...])
    if final:
        y = y * _rms_scale(y) * fn_ref[...]
    o_ref[...] = y


def _ple(h, nw, wg, p, wp, fn, *, tm, final):
    t, d = h.shape
    kp = p.shape[1]
    return pl.pallas_call(
        functools.partial(_ple_kernel, final=final),
        grid=(t // tm,),
        in_specs=[pl.BlockSpec((tm, d), lambda i: (i, 0)),
                  pl.BlockSpec((1, d), lambda i: (0, 0)),
                  pl.BlockSpec((d, d), lambda i: (0, 0)),
                  pl.BlockSpec((tm, kp), lambda i: (i, 0)),
                  pl.BlockSpec((kp, d), lambda i: (0, 0)),
                  pl.BlockSpec((1, d), lambda i: (0, 0))],
        out_specs=pl.BlockSpec((tm, d), lambda i: (i, 0)),
        out_shape=jax.ShapeDtypeStruct((t, d), F32),
        compiler_params=_cparams("parallel"),
        name="ple",
    )(h, nw.reshape(1, d), wg, p, wp, fn.reshape(1, d))


def _gla_levels(c):
    out, b = [], c // 2
    while b >= DIAG:
        out.append(b)
        b //= 2
    return out


def _gla_consts(c):
    idx = np.arange(c)
    i, m = idx[:, None], idx[None, :]
    blocks = [(m <= i), (m > i)]
    masks = []
    for b in _gla_levels(c):
        mid = (idx // (2 * b)) * (2 * b) + b - 1
        md = mid[:, None]
        blocks.append(np.where(i > md, (m > md) & (m <= i), (m > i) & (m <= md)))
        masks.append((i // (2 * b) == m // (2 * b)) & (i % (2 * b) >= b) & (m % (2 * b) < b))
    masks.append((i // DIAG == m // DIAG) & (m <= i))
    fwd_w = np.concatenate([bl.astype(np.float32) for bl in blocks], axis=0)
    bwd_w = np.concatenate([bl.astype(np.float32)[::-1, ::-1] for bl in blocks], axis=0)
    ones = np.ones((V7X_SUBLANES, c), np.float32)
    wcum = np.stack([np.concatenate([bwd_w, ones], 0), np.concatenate([fwd_w, ones], 0)])
    fwd_m = np.stack([mk.astype(np.float32) for mk in masks])
    bwd_m = np.stack([mk.astype(np.float32)[::-1, ::-1] for mk in masks])
    mask = np.stack([bwd_m, fwd_m])
    sel = (idx[None, None, :] % DIAG == np.arange(DIAG)[:, None, None])
    sel = np.broadcast_to(sel, (DIAG, GLA_DK, c)).astype(np.float32)
    return wcum, mask, sel


def _bcast_row_in_group(x, jj):
    c, n = x.shape
    x3 = x.reshape(c // DIAG, DIAG, n)
    return jnp.broadcast_to(x3[:, jj:jj + 1, :], (c // DIAG, DIAG, n)).reshape(c, n)


def _gla_kernel(q_ref, k_ref, v_ref, og_ref, lr_ref, wgu_ref, bg_ref, wcum_ref, mask_ref,
                sel_ref, onorm_ref, o_ref, st_ref, obwd_ref, qd_ref, oa_ref, dl_ref, et_ref,
                *, c, nci, tb):
    d = pl.program_id(1)
    b = pl.program_id(2)
    nb = pl.num_programs(2)
    nlev = len(_gla_levels(c))

    @pl.when(b == 0)
    def _():
        st_ref[...] = jnp.zeros_like(st_ref)

    def prepare():
        chunks = range(nci)
        rows = [slice(ci * c, (ci + 1) * c) for ci in chunks]
        q = [q_ref[r, :] * (GLA_DK ** -0.5) for r in rows]
        k = [k_ref[r, :] for r in rows]
        vb = [v_ref[r, :].astype(BF16) for r in rows]
        gk = [_dot(lr_ref[r, :].astype(BF16), wgu_ref[...]) + bg_ref[...] for r in rows]
        g = [(jnp.minimum(x, 0.0) - jnp.log(1.0 + jnp.exp(-jnp.abs(x)))) * (1.0 / GLA_GATE_NORM) for x in gk]
        gp = [jnp.concatenate(_split3(x), axis=1) for x in g]
        cums = [_dot(wcum_ref[...], x) for x in gp]
        cums = [x[:, :GLA_DK] + x[:, GLA_DK:2 * GLA_DK] + x[:, 2 * GLA_DK:] for x in cums]
        cum = [x[0:c] for x in cums]
        rest = [x[c:2 * c] for x in cums]
        for ci in chunks:
            et_ref[ci] = jnp.exp(cums[ci][(2 + nlev) * c:(2 + nlev) * c + V7X_SUBLANES])

        a = [jnp.zeros((c, c), F32) for _ in chunks]
        for lv in range(nlev):
            e = [jnp.exp(x[(2 + lv) * c:(3 + lv) * c]) for x in cums]
            s = [_dot_nt((qq * ee).astype(BF16), (kk * ee).astype(BF16)) for qq, kk, ee in zip(q, k, e)]
            a = [aa + ss * mask_ref[lv] for aa, ss in zip(a, s)]
        ad = [jnp.zeros((c, c), F32) for _ in chunks]
        for jj in range(DIAG):
            tj = [qq * _bcast_row_in_group(kk, jj)
                  * jnp.exp(jnp.minimum(cc - _bcast_row_in_group(cc, jj), 0.0))
                  for qq, kk, cc in zip(q, k, cum)]
            ad = [x + _dot(t.astype(BF16), sel_ref[jj]) for x, t in zip(ad, tj)]
        a = [(aa + x * mask_ref[nlev]).astype(BF16) for aa, x in zip(a, ad)]

        kd = [(kk * jnp.exp(rr)).astype(BF16) for kk, rr in zip(k, rest)]
        for ci in chunks:
            qd_ref[ci] = (q[ci] * jnp.exp(cum[ci])).astype(BF16)
            oa_ref[ci] = _dot(a[ci], vb[ci])
            dl_ref[ci] = _dot_tn(vb[ci], kd[ci])

    def walk(order, blk, forward):
        for ci in order:
            rows = slice(ci * c, (ci + 1) * c)
            st = st_ref[...]
            o = _dot_nt(qd_ref[ci], st.astype(BF16)) + oa_ref[ci]
            st_ref[...] = st * et_ref[ci][0:1] + dl_ref[ci]
            orow = pl.ds(pl.multiple_of(blk * tb + ci * c, c), c)
            if forward:
                ot = o + obwd_ref[orow, :]
                y = ot * _rms_scale(ot) * onorm_ref[...]
                og = og_ref[rows, :]
                o_ref[rows, :] = (y * (og * _sigmoid(og))).astype(o_ref.dtype)
            else:
                obwd_ref[orow, :] = o

    prepare()

    @pl.when(d == 0)
    def _():
        walk(range(nci - 1, -1, -1), nb - 1 - b, False)

    @pl.when(d == 1)
    def _():
        walk(range(nci), b, True)


def _gla_scan(proj, wgu, bg, onorm, *, t):
    c = SCAN_CHUNK
    tb = min(t, 512)
    nb = t // tb
    nci = tb // c
    wcum, mask, sel = _gla_consts(c)
    nlev = len(_gla_levels(c))
    qoff = 0
    koff = GLA_HEADS * GLA_DK // GLA_DK
    voff = 2 * GLA_HEADS * GLA_DK // GLA_DV
    goff = voff + GLA_HEADS
    lroff = (2 * GLA_HEADS * GLA_DK + 2 * GLA_HEADS * GLA_DV) // V7X_LANES

    def tblk(d, b):
        return d * b + (1 - d) * (nb - 1 - b)

    return pl.pallas_call(
        functools.partial(_gla_kernel, c=c, nci=nci, tb=tb),
        grid=(GLA_HEADS, 2, nb),
        in_specs=[
            pl.BlockSpec((tb, GLA_DK), lambda h, d, b: (tblk(d, b), qoff + h)),
            pl.BlockSpec((tb, GLA_DK), lambda h, d, b: (tblk(d, b), koff + h)),
            pl.BlockSpec((tb, GLA_DV), lambda h, d, b: (tblk(d, b), voff + h)),
            pl.BlockSpec((tb, GLA_DV), lambda h, d, b: (tblk(d, b), goff + h)),
            pl.BlockSpec((tb, V7X_LANES), lambda h, d, b: (tblk(d, b), lroff)),
            pl.BlockSpec((None, None, V7X_LANES, GLA_DK), lambda h, d, b: (d, h, 0, 0)),
            pl.BlockSpec((None, None, 1, GLA_DK), lambda h, d, b: (d, h, 0, 0)),
            pl.BlockSpec((None, wcum.shape[1], c), lambda h, d, b: (d, 0, 0)),
            pl.BlockSpec((None, nlev + 1, c, c), lambda h, d, b: (d, 0, 0, 0)),
            pl.BlockSpec((DIAG, GLA_DK, c), lambda h, d, b: (0, 0, 0)),
            pl.BlockSpec((1, GLA_DV), lambda h, d, b: (0, 0)),
        ],
        out_specs=pl.BlockSpec((tb, GLA_DV), lambda h, d, b: (d * b, h)),
        out_shape=jax.ShapeDtypeStruct((t, GLA_HEADS * GLA_DV), BF16),
        scratch_shapes=[pltpu.VMEM((GLA_DV, GLA_DK), F32),
                        pltpu.VMEM((t, GLA_DV), F32),
                        pltpu.VMEM((nci, c, GLA_DK), BF16),
                        pltpu.VMEM((nci, c, GLA_DV), F32),
                        pltpu.VMEM((nci, GLA_DV, GLA_DK), F32),
                        pltpu.VMEM((nci, V7X_SUBLANES, GLA_DK), F32)],
        compiler_params=_cparams("arbitrary", "arbitrary", "arbitrary"),
        name="gla_scan",
    )(proj, proj, proj, proj, proj, wgu, bg,
      jnp.asarray(wcum, BF16), jnp.asarray(mask, F32), jnp.asarray(sel, BF16),
      onorm.reshape(1, GLA_DV))


def _gla_gate_weights(w_gate_up, b_gate):
    w = w_gate_up.reshape(2, GLA_RANK, GLA_HEADS, GLA_DK).transpose(0, 2, 1, 3)
    z = jnp.zeros((GLA_HEADS, V7X_LANES, GLA_DK), F32)
    fwd = z.at[:, 0:GLA_RANK].set(w[0])
    bwd = z.at[:, GLA_RANK:2 * GLA_RANK].set(w[1])
    wgu = jnp.stack([bwd, fwd]).astype(BF16)
    bg = jnp.stack([b_gate[1], b_gate[0]]).reshape(2, GLA_HEADS, 1, GLA_DK)
    return wgu, bg


def _gdn_prep_kernel(prev_ref, cur_ref, next_ref, cw_ref, o_ref, xp_ref, *, tb, l2, nq):
    i = pl.program_id(0)
    j = pl.program_id(1)
    halo = V7X_SUBLANES
    xp_ref[0:halo, :] = jnp.where(i > 0, prev_ref[...], 0.0)
    xp_ref[halo:halo + tb, :] = cur_ref[...]
    xp_ref[halo + tb:2 * halo + tb, :] = jnp.where(i < pl.num_programs(0) - 1, next_ref[...], 0.0)
    acc = jnp.zeros(cur_ref.shape, F32)
    for w in range(CONV_W):
        acc = acc + xp_ref[pl.ds(halo - CONV_W // 2 + w, tb), :] * cw_ref[w:w + 1, :]
    y = acc * _sigmoid(acc)
    if l2:
        qs = jnp.where(j < nq, GDN_DK ** -0.5, 1.0)
        for hh in range(cur_ref.shape[1] // GDN_DK):
            seg = y[:, hh * GDN_DK:(hh + 1) * GDN_DK]
            ss = jnp.sum(seg * seg, axis=-1, keepdims=True)
            o_ref[:, hh * GDN_DK:(hh + 1) * GDN_DK] = seg * (lax.rsqrt(ss + EPS) * qs)
    else:
        o_ref[...] = y


def _gdn_prep(proj, conv_w, *, t, col0, ncols, l2):
    tb = min(t, 512)
    cb = 512
    hb = tb // V7X_SUBLANES
    nt = t // tb
    c0 = col0 // cb
    nq = GDN_QK_HEADS * GDN_DK // cb
    return pl.pallas_call(
        functools.partial(_gdn_prep_kernel, tb=tb, l2=l2, nq=nq),
        grid=(nt, ncols // cb),
        in_specs=[
            pl.BlockSpec((V7X_SUBLANES, cb), lambda i, j: (jnp.maximum(i * hb - 1, 0), c0 + j)),
            pl.BlockSpec((tb, cb), lambda i, j: (i, c0 + j)),
            pl.BlockSpec((V7X_SUBLANES, cb), lambda i, j: (jnp.minimum((i + 1) * hb, nt * hb - 1), c0 + j)),
            pl.BlockSpec((CONV_W, cb), lambda i, j: (0, c0 + j)),
        ],
        out_specs=pl.BlockSpec((tb, cb), lambda i, j: (i, j)),
        out_shape=jax.ShapeDtypeStruct((t, ncols), F32),
        scratch_shapes=[pltpu.VMEM((tb + 2 * V7X_SUBLANES, cb), F32)],
        compiler_params=_cparams("parallel", "parallel"),
        name="gdn_prep",
    )(proj, proj, proj, conv_w)


def _gdn_consts(c):
    idx = np.arange(c)
    i, m = idx[:, None], idx[None, :]
    u_f = (i <= m)
    uv_f = np.concatenate([u_f, ~u_f], axis=1).astype(np.float32)
    u_b = (i >= m)
    uv_b = np.concatenate([u_b, ~u_b], axis=1).astype(np.float32)
    uv = np.stack([uv_b, uv_f])

    def pair(b):
        return (i // (2 * b) == m // (2 * b)) & (i // b != m // b)

    lower, strict = (m <= i), (m < i)
    blk = (i // 16 == m // 16)
    levels = []
    b = 16
    while 2 * b <= c:
        levels.append(pair(b))
        b *= 2
    fwd = [lower, strict, blk & strict] + [p & strict for p in levels]
    mask_f = np.stack([x.astype(np.float32) for x in fwd])
    mask_b = np.stack([x.astype(np.float32)[::-1, ::-1] for x in fwd])
    mask = np.stack([mask_b, mask_f])
    return uv, mask, np.eye(c, dtype=np.float32), len(levels)


def _gdn_kernel(q_ref, k_ref, v_ref, z_ref, gt_ref, alog_ref, dtb_ref, uv_ref, mask_ref,
                eye_ref, onorm_ref, o_ref, s_ref, obwd_ref, mq_ref, bo_ref, et_ref,
                *, c, nci, tb, nlev):
    d = pl.program_id(1)
    b = pl.program_id(2)
    nb = pl.num_programs(2)

    @pl.when(b == 0)
    def _():
        s_ref[...] = jnp.zeros_like(s_ref)

    row8 = lax.broadcasted_iota(jnp.int32, (V7X_SUBLANES, c), 0)

    def prepare():
        chunks = range(nci)
        units = [(ci, hh) for ci in chunks for hh in range(2)]
        rows = [slice(ci * c, (ci + 1) * c) for ci in chunks]
        gts = [gt_ref[ci] for ci in chunks]
        beta8 = [_sigmoid(gt) for gt in gts]
        g8 = [-jnp.exp(alog_ref[...]) * _softplus(gt + dtb_ref[...]) for gt in gts]
        gp = [jnp.concatenate(_split3(g), axis=0) for g in g8]
        qf = [q_ref[r, :] for r in rows]
        kf = [k_ref[r, :] for r in rows]
        kb = [k.astype(BF16) for k in kf]
        cl = [_dot(g, uv_ref[...]) for g in gp]
        kk = [_dot_nt(k, k) for k in kb]
        qk = [_dot_nt(q.astype(BF16), k) for q, k in zip(qf, kb)]
        kt = [k.T for k in kf]
        cl = [x[0:8] + x[8:16] + x[16:24] for x in cl]
        cum8 = [x[:, :c] for x in cl]
        rest8 = [x[:, c:] for x in cl]
        cols = [jnp.concatenate([jnp.where(row8 < 2, bt, cm), jnp.zeros((c - V7X_SUBLANES, c), F32)],
                                axis=0).T for bt, cm in zip(beta8, cum8)]
        for ci in chunks:
            et_ref[ci] = jnp.exp(cum8[ci] + rest8[ci])

        beta_c = [cols[ci][:, hh:hh + 1] for ci, hh in units]
        cum_c = [cols[ci][:, 2 + hh:3 + hh] for ci, hh in units]
        dec = [jnp.exp(jnp.minimum(cc - cum8[ci][2 + hh:3 + hh, :], 0.0))
               for cc, (ci, hh) in zip(cum_c, units)]
        lm = [kk[ci] * dc * bc * mask_ref[1] for dc, bc, (ci, hh) in zip(dec, beta_c, units)]
        at = [(qk[ci] * dc * mask_ref[0]).astype(BF16) for dc, (ci, hh) in zip(dec, units)]

        n1 = [-(l * mask_ref[2]) for l in lm]
        x = [eye_ref[...] + n for n in n1]
        p = [n.astype(BF16) for n in n1]
        for _ in range(3):
            p = [_dot(pp, pp).astype(BF16) for pp in p]
            x = [xx + _dot(xx.astype(BF16), pp) for xx, pp in zip(x, p)]
        for lv in range(nlev):
            xb = [xx.astype(BF16) for xx in x]
            t1 = [_dot((l * mask_ref[3 + lv]).astype(BF16), b).astype(BF16) for l, b in zip(lm, xb)]
            x = [xx - _dot(b, t) for xx, b, t in zip(x, xb, t1)]

        e_c = [jnp.exp(cc) for cc in cum_c]
        rhs = [jnp.concatenate([v_ref[rows[ci], hh * GDN_DV:(hh + 1) * GDN_DV] * bc, kf[ci] * (bc * ec)],
                               axis=1).astype(BF16)
               for bc, ec, (ci, hh) in zip(beta_c, e_c, units)]
        uw = [_dot(xx.astype(BF16), r).astype(BF16) for xx, r in zip(x, rhs)]
        kdt = [(kt[ci] * jnp.exp(rest8[ci][2 + hh:3 + hh, :])).astype(BF16) for ci, hh in units]
        pr = [_dot(jnp.concatenate([kd, a], axis=0), w) for kd, a, w in zip(kdt, at, uw)]
        for n, (ci, hh) in enumerate(units):
            m = -pr[n][:GDN_DK, GDN_DV:]
            qm = qf[ci] * e_c[n] - pr[n][GDN_DK:, GDN_DV:]
            mq_ref[ci, hh] = jnp.concatenate([m, qm], axis=0).astype(BF16)
            bo_ref[ci, hh] = pr[n][:, :GDN_DV]

    def step(ci, blk, forward):
        rows = slice(ci * c, (ci + 1) * c)
        et = et_ref[ci]
        for hh in range(2):
            s = s_ref[hh]
            r = _dot(mq_ref[ci, hh], s.astype(BF16)) + bo_ref[ci, hh]
            s_ref[hh] = s * et[2 + hh:3 + hh, :] + r[:GDN_DK]
            o = r[GDN_DK:]
            orow = pl.ds(pl.multiple_of(blk * tb + ci * c, c), c)
            ocol = slice(hh * GDN_DV, (hh + 1) * GDN_DV)
            if forward:
                ot = o + obwd_ref[orow, ocol]
                y = ot * _rms_scale(ot) * onorm_ref[...]
                z = z_ref[rows, ocol]
                o_ref[rows, ocol] = (y * (z * _sigmoid(z))).astype(o_ref.dtype)
            else:
                obwd_ref[orow, ocol] = o

    prepare()

    @pl.when(d == 0)
    def _():
        for i in range(nci):
            step(nci - 1 - i, nb - 1 - b, False)

    @pl.when(d == 1)
    def _():
        for i in range(nci):
            step(i, b, True)


def _gdn_scan(qkn, vv, proj, gt, alog, dtb, onorm, *, t):
    c = SCAN_CHUNK
    tb = min(t, 512)
    nb = t // tb
    nci = tb // c
    uv, mask, eye, nlev = _gdn_consts(c)
    hw = 2 * GDN_DV
    zoff = (2 * GDN_QK_HEADS * GDN_DK + 2 * GDN_QK_HEADS * GDN_DV) // hw

    def tblk(d, b):
        return d * b + (1 - d) * (nb - 1 - b)

    return pl.pallas_call(
        functools.partial(_gdn_kernel, c=c, nci=nci, tb=tb, nlev=nlev),
        grid=(GDN_QK_HEADS, 2, nb),
        in_specs=[
            pl.BlockSpec((tb, GDN_DK), lambda h, d, b: (tblk(d, b), h)),
            pl.BlockSpec((tb, GDN_DK), lambda h, d, b: (tblk(d, b), GDN_QK_HEADS + h)),
            pl.BlockSpec((tb, hw), lambda h, d, b: (tblk(d, b), h)),
            pl.BlockSpec((tb, hw), lambda h, d, b: (tblk(d, b), zoff + h)),
            pl.BlockSpec((None, None, nci, V7X_SUBLANES, c), lambda h, d, b: (d, h, tblk(d, b), 0, 0)),
            pl.BlockSpec((None, None, V7X_SUBLANES, c), lambda h, d, b: (d, h, 0, 0)),
            pl.BlockSpec((None, None, V7X_SUBLANES, c), lambda h, d, b: (d, h, 0, 0)),
            pl.BlockSpec((None, c, 2 * c), lambda h, d, b: (d, 0, 0)),
            pl.BlockSpec((None, mask.shape[1], c, c), lambda h, d, b: (d, 0, 0, 0)),
            pl.BlockSpec((c, c), lambda h, d, b: (0, 0)),
            pl.BlockSpec((1, GDN_DV), lambda h, d, b: (0, 0)),
        ],
        out_specs=pl.BlockSpec((tb, hw), lambda h, d, b: (d * b, h)),
        out_shape=jax.ShapeDtypeStruct((t, 2 * GDN_QK_HEADS * GDN_DV), BF16),
        scratch_shapes=[pltpu.VMEM((2, GDN_DK, GDN_DV), F32),
                        pltpu.VMEM((t, hw), F32),
                        pltpu.VMEM((nci, 2, GDN_DK + c, GDN_DK), BF16),
                        pltpu.VMEM((nci, 2, GDN_DK + c, GDN_DV), F32),
                        pltpu.VMEM((nci, V7X_SUBLANES, c), F32)],
        compiler_params=_cparams("arbitrary", "arbitrary", "arbitrary"),
        name="gdn_scan",
    )(qkn, qkn, vv, proj, gt, alog, dtb,
      jnp.asarray(uv, BF16), jnp.asarray(mask, F32), jnp.asarray(eye, F32),
      onorm.reshape(1, GDN_DV))


def _gdn_gate_layout(ba, a_log, dt_bias, *, t):
    c = SCAN_CHUNK
    g = ba.reshape(t, 2, 2, GDN_QK_HEADS, 2)
    g = g.transpose(2, 3, 1, 4, 0).reshape(2, GDN_QK_HEADS, 4, t)
    g = jnp.concatenate([g, jnp.zeros_like(g)], axis=2)
    g = g.reshape(2, GDN_QK_HEADS, V7X_SUBLANES, t // c, c).transpose(0, 1, 3, 2, 4)
    g = jnp.stack([g[1], g[0]])

    def rows(pv):
        pv = pv.reshape(2, GDN_QK_HEADS, 2)
        z = jnp.zeros_like(pv)
        r = jnp.concatenate([z, pv, z, z], axis=2)
        r = jnp.broadcast_to(r[..., None], (2, GDN_QK_HEADS, V7X_SUBLANES, c))
        return jnp.stack([r[1], r[0]])

    return g, rows(a_log.astype(F32)), rows(dt_bias.astype(F32))


def _pad_cols(w, n):
    return jnp.concatenate([w, jnp.zeros((w.shape[0], n - w.shape[1]), w.dtype)], axis=1)


def kernel(x, p, mixer_norm, gla_w_in, gla_w_gate_up, gla_b_gate, gla_out_norm, gla_w_out,
           gdn_w_in, gdn_conv, gdn_a_log, gdn_dt_bias, gdn_out_norm, gdn_w_out,
           ffn_norm, ffn_w_in, ffn_w_out, ple_norm, ple_w_gate, ple_w_proj, final_norm):
    _, t, dm = x.shape
    depth = mixer_norm.shape[0]
    h = x.reshape(t, dm)
    tm = min(t, 1024)
    for i in range(depth):
        j = i // 2
        if i % 2 == 0:
            n_pad = -(-gla_w_in.shape[2] // PROJ_TN) * PROJ_TN
            w_in = _pad_cols(gla_w_in[j].astype(BF16), n_pad)
            proj = _norm_matmul(h, mixer_norm[i], w_in, tm=tm, tn=PROJ_TN)
            wgu, bg = _gla_gate_weights(gla_w_gate_up[j], gla_b_gate[j])
            o = _gla_scan(proj, wgu, bg, gla_out_norm[j], t=t)
            h = _matmul_res(o, gla_w_out[j].astype(BF16), h, tm=tm, tn=512)
        else:
            n_pad = -(-gdn_w_in.shape[2] // PROJ_TN) * PROJ_TN
            w_in = _pad_cols(gdn_w_in[j].astype(BF16), n_pad)
            proj = _norm_matmul(h, mixer_norm[i], w_in, tm=tm, tn=PROJ_TN)
            kd = GDN_QK_HEADS * GDN_DK
            vd = 2 * GDN_QK_HEADS * GDN_DV
            qkn = _gdn_prep(proj, gdn_conv[j], t=t, col0=0, ncols=2 * kd, l2=True)
            vv = _gdn_prep(proj, gdn_conv[j], t=t, col0=2 * kd, ncols=vd, l2=False)
            ba = proj[:, 2 * kd + 2 * vd:2 * kd + 2 * vd + 4 * 2 * GDN_QK_HEADS]
            gt, alog, dtb = _gdn_gate_layout(ba, gdn_a_log[j], gdn_dt_bias[j], t=t)
            o = _gdn_scan(qkn, vv, proj, gt, alog, dtb, gdn_out_norm[j], t=t)
            h = _matmul_res(o, gdn_w_out[j].astype(BF16), h, tm=tm, tn=512)
        h = _ffn(h, ffn_norm[i], ffn_w_in[i].astype(BF16), ffn_w_out[i].astype(BF16),
                 tm=min(t, 512), tf=512)
        h = _ple(h, ple_norm[i], ple_w_gate[i].astype(BF16), p[i].reshape(t, -1),
                 ple_w_proj[i].astype(BF16), final_norm, tm=min(t, 256), final=(i == depth - 1))
    return h.reshape(x.shape)
```

```python
import functools

import numpy as np
import jax
import jax.numpy as jnp
from jax import lax
from jax.experimental import pallas as pl
from jax.experimental.pallas import tpu as pltpu

F32 = jnp.float32
BF16 = jnp.bfloat16
EPS = 1e-6

V7X_LANES = 128
V7X_SUBLANES = 8
V7X_VMEM_LIMIT_BYTES = 56 * 1024 * 1024

GLA_HEADS = 4
GLA_DK = 256
GLA_DV = 512
GLA_RANK = 16
GLA_GATE_NORM = 16.0
GDN_QK_HEADS = 16
GDN_DK = 128
GDN_DV = 128
CONV_W = 5
SCAN_CHUNK = 128
DIAG = V7X_SUBLANES
V7X_MXU_COLS = 256
PROJ_TN_MAX = 1792


def _cparams(*sem):
    return pltpu.CompilerParams(dimension_semantics=sem,
                                vmem_limit_bytes=V7X_VMEM_LIMIT_BYTES)


def _dot(a, b):
    return jnp.dot(a, b, preferred_element_type=F32)


def _dot_nt(a, b):
    return lax.dot_general(a, b, (((1,), (1,)), ((), ())), preferred_element_type=F32)


def _dot_tn(a, b):
    return lax.dot_general(a, b, (((0,), (0,)), ((), ())), preferred_element_type=F32)


def _rms_scale(x):
    return lax.rsqrt(jnp.mean(x * x, axis=-1, keepdims=True) + EPS)


def _sigmoid(x):
    return 1.0 / (1.0 + jnp.exp(-x))


def _softplus(x):
    return jnp.maximum(x, 0.0) + jnp.log(1.0 + jnp.exp(-jnp.abs(x)))


def _split3(x):
    p1 = x.astype(BF16)
    r = x - p1.astype(F32)
    p2 = r.astype(BF16)
    p3 = (r - p2.astype(F32)).astype(BF16)
    return p1, p2, p3


def _norm_matmul_kernel(x_ref, nw_ref, w_ref, o_ref, xn_ref):
    @pl.when(pl.program_id(1) == 0)
    def _():
        x = x_ref[...]
        xn_ref[...] = (x * _rms_scale(x) * nw_ref[...]).astype(BF16)

    o_ref[...] = _dot(xn_ref[...], w_ref[...]).astype(o_ref.dtype)


def _norm_matmul(x, nw, w, *, tm, tn):
    t, d = x.shape
    n = w.shape[1]
    return pl.pallas_call(
        _norm_matmul_kernel,
        grid=(t // tm, n // tn),
        in_specs=[pl.BlockSpec((tm, d), lambda i, j: (i, 0)),
                  pl.BlockSpec((1, d), lambda i, j: (0, 0)),
                  pl.BlockSpec((d, tn), lambda i, j: (0, j))],
        out_specs=pl.BlockSpec((tm, tn), lambda i, j: (i, j)),
        out_shape=jax.ShapeDtypeStruct((t, n), F32),
        scratch_shapes=[pltpu.VMEM((tm, d), BF16)],
        compiler_params=_cparams("parallel", "arbitrary"),
        name="norm_matmul",
    )(x, nw.reshape(1, d), w)


def _matmul_res_kernel(a_ref, w_ref, h_ref, o_ref):
    o_ref[...] = h_ref[...] + _dot(a_ref[...], w_ref[...])


def _matmul_res(a, w, h, *, tm, tn):
    t, k = a.shape
    n = w.shape[1]
    return pl.pallas_call(
        _matmul_res_kernel,
        grid=(t // tm, n // tn),
        in_specs=[pl.BlockSpec((tm, k), lambda i, j: (i, 0)),
                  pl.BlockSpec((k, tn), lambda i, j: (0, j)),
                  pl.BlockSpec((tm, tn), lambda i, j: (i, j))],
        out_specs=pl.BlockSpec((tm, tn), lambda i, j: (i, j)),
        out_shape=jax.ShapeDtypeStruct((t, n), F32),
        compiler_params=_cparams("parallel", "arbitrary"),
        name="matmul_res",
    )(a, w, h)


def _ffn_kernel(x_ref, nw_ref, wg_ref, wu_ref, wo_ref, o_ref, xn_ref):
    @pl.when(pl.program_id(1) == 0)
    def _():
        x = x_ref[...]
        xn_ref[...] = (x * _rms_scale(x) * nw_ref[...]).astype(BF16)
        o_ref[...] = x

    xn = xn_ref[...]
    g = _dot(xn, wg_ref[...])
    u = _dot(xn, wu_ref[...])
    act = (g * _sigmoid(g) * u).astype(BF16)
    o_ref[...] += _dot(act, wo_ref[...])


def _ffn(x, nw, w_in, w_out, *, tm, tf):
    t, d = x.shape
    f = w_out.shape[0]
    nf = f // tf
    return pl.pallas_call(
        _ffn_kernel,
        grid=(t // tm, nf),
        in_specs=[pl.BlockSpec((tm, d), lambda i, j: (i, 0), pipeline_mode=pl.Buffered(1)),
                  pl.BlockSpec((1, d), lambda i, j: (0, 0)),
                  pl.BlockSpec((d, tf), lambda i, j: (0, j)),
                  pl.BlockSpec((d, tf), lambda i, j: (0, nf + j)),
                  pl.BlockSpec((tf, d), lambda i, j: (j, 0))],
        out_specs=pl.BlockSpec((tm, d), lambda i, j: (i, 0)),
        out_shape=jax.ShapeDtypeStruct((t, d), F32),
        scratch_shapes=[pltpu.VMEM((tm, d), BF16)],
        compiler_params=_cparams("parallel", "arbitrary"),
        name="ffn",
    )(x, nw.reshape(1, d), w_in, w_in, w_out)


def _ple_kernel(h_ref, nw_ref, wg_ref, p_ref, wp_ref, fn_ref, o_ref, *, final):
    h = h_ref[...]
    xn = (h * _rms_scale(h) * nw_ref[...]).astype(BF16)
    gate = _sigmoid(_dot(xn, wg_ref[...]))
    y = h + gate * _dot(p_ref[...].astype(BF16), wp_ref[...])
    if final:
        y = y * _rms_scale(y) * fn_ref[...]
    o_ref[...] = y


def _ple(h, nw, wg, p, wp, fn, *, tm, final):
    t, d = h.shape
    kp = p.shape[1]
    return pl.pallas_call(
        functools.partial(_ple_kernel, final=final),
        grid=(t // tm,),
        in_specs=[pl.BlockSpec((tm, d), lambda i: (i, 0)),
                  pl.BlockSpec((1, d), lambda i: (0, 0)),
                  pl.BlockSpec((d, d), lambda i: (0, 0), pipeline_mode=pl.Buffered(1)),
                  pl.BlockSpec((tm, kp), lambda i: (i, 0)),
                  pl.BlockSpec((kp, d), lambda i: (0, 0), pipeline_mode=pl.Buffered(1)),
                  pl.BlockSpec((1, d), lambda i: (0, 0))],
        out_specs=pl.BlockSpec((tm, d), lambda i: (i, 0)),
        out_shape=jax.ShapeDtypeStruct((t, d), F32),
        compiler_params=_cparams("parallel"),
        name="ple",
    )(h, nw.reshape(1, d), wg, p, wp, fn.reshape(1, d))


def _gla_levels(c):
    out, b = [], c // 2
    while b >= DIAG:
        out.append(b)
        b //= 2
    return out


def _gla_consts(c):
    idx = np.arange(c)
    i, m = idx[:, None], idx[None, :]
    blocks = [(m <= i), (m > i)]
    masks = []
    for b in _gla_levels(c):
        mid = (idx // (2 * b)) * (2 * b) + b - 1
        md = mid[:, None]
        blocks.append(np.where(i > md, (m > md) & (m <= i), (m > i) & (m <= md)))
        masks.append((i // (2 * b) == m // (2 * b)) & (i % (2 * b) >= b) & (m % (2 * b) < b))
    masks.append((i // DIAG == m // DIAG) & (m <= i))
    fwd_w = np.concatenate([bl.astype(np.float32) for bl in blocks], axis=0)
    bwd_w = np.concatenate([bl.astype(np.float32)[::-1, ::-1] for bl in blocks], axis=0)
    ones = np.ones((V7X_SUBLANES, c), np.float32)
    wcum = np.stack([np.concatenate([bwd_w, ones], 0), np.concatenate([fwd_w, ones], 0)])
    fwd_m = np.stack([mk.astype(np.float32) for mk in masks])
    bwd_m = np.stack([mk.astype(np.float32)[::-1, ::-1] for mk in masks])
    mask = np.stack([bwd_m, fwd_m])
    sel = (idx[None, None, :] % DIAG == np.arange(DIAG)[:, None, None])
    sel = np.broadcast_to(sel, (DIAG, GLA_DK, c)).astype(np.float32)
    return wcum, mask, sel


def _bcast_row_in_group(x, jj):
    c, n = x.shape
    x3 = x.reshape(c // DIAG, DIAG, n)
    return jnp.broadcast_to(x3[:, jj:jj + 1, :], (c // DIAG, DIAG, n)).reshape(c, n)


def _gla_kernel(q_ref, k_ref, v_ref, og_ref, lr_ref, wgu_ref, bg_ref, wcum_ref, mask_ref,
                sel_ref, onorm_ref, o_ref, st_ref, obwd_ref, qd_ref, oa_ref, dl_ref, et_ref,
                *, c, nci, tb):
    d = pl.program_id(1)
    b = pl.program_id(2)
    nb = pl.num_programs(2)
    nlev = len(_gla_levels(c))

    @pl.when(b == 0)
    def _():
        st_ref[...] = jnp.zeros_like(st_ref)

    def prepare():
        chunks = range(nci)
        rows = [slice(ci * c, (ci + 1) * c) for ci in chunks]
        q = [q_ref[r, :] * (GLA_DK ** -0.5) for r in rows]
        k = [k_ref[r, :] for r in rows]
        vb = [v_ref[r, :].astype(BF16) for r in rows]
        gk = [_dot(lr_ref[r, :].astype(BF16), wgu_ref[...]) + bg_ref[...] for r in rows]
        g = [(jnp.minimum(x, 0.0) - jnp.log(1.0 + jnp.exp(-jnp.abs(x)))) * (1.0 / GLA_GATE_NORM) for x in gk]
        gp = [jnp.concatenate(_split3(x), axis=1) for x in g]
        cums = [_dot(wcum_ref[...], x) for x in gp]
        cums = [x[:, :GLA_DK] + x[:, GLA_DK:2 * GLA_DK] + x[:, 2 * GLA_DK:] for x in cums]
        cum = [x[0:c] for x in cums]
        rest = [x[c:2 * c] for x in cums]
        for ci in chunks:
            et_ref[ci] = jnp.exp(cums[ci][(2 + nlev) * c:(2 + nlev) * c + V7X_SUBLANES])

        a = [jnp.zeros((c, c), F32) for _ in chunks]
        for lv in range(nlev):
            e = [jnp.exp(x[(2 + lv) * c:(3 + lv) * c]) for x in cums]
            s = [_dot_nt((qq * ee).astype(BF16), (kk * ee).astype(BF16)) for qq, kk, ee in zip(q, k, e)]
            a = [aa + ss * mask_ref[lv] for aa, ss in zip(a, s)]
        ad = [jnp.zeros((c, c), F32) for _ in chunks]
        for jj in range(DIAG):
            tj = [qq * _bcast_row_in_group(kk, jj)
                  * jnp.exp(jnp.minimum(cc - _bcast_row_in_group(cc, jj), 0.0))
                  for qq, kk, cc in zip(q, k, cum)]
            ad = [x + _dot(t.astype(BF16), sel_ref[jj]) for x, t in zip(ad, tj)]
        a = [(aa + x * mask_ref[nlev]).astype(BF16) for aa, x in zip(a, ad)]

        kd = [(kk * jnp.exp(rr)).astype(BF16) for kk, rr in zip(k, rest)]
        for ci in chunks:
            qd_ref[ci] = (q[ci] * jnp.exp(cum[ci])).astype(BF16)
            oa_ref[ci] = _dot(a[ci], vb[ci])
            dl_ref[ci] = _dot_tn(vb[ci], kd[ci])

    def walk(order, blk, forward):
        for ci in order:
            rows = slice(ci * c, (ci + 1) * c)
            st = st_ref[...]
            o = _dot_nt(qd_ref[ci], st.astype(BF16)) + oa_ref[ci]
            st_ref[...] = st * et_ref[ci][0:1] + dl_ref[ci]
            orow = pl.ds(pl.multiple_of(blk * tb + ci * c, c), c)
            if forward:
                ot = o + obwd_ref[orow, :]
                y = ot * _rms_scale(ot) * onorm_ref[...]
                og = og_ref[rows, :]
                o_ref[rows, :] = (y * (og * _sigmoid(og))).astype(o_ref.dtype)
            else:
                obwd_ref[orow, :] = o

    prepare()

    @pl.when(d == 0)
    def _():
        walk(range(nci - 1, -1, -1), nb - 1 - b, False)

    @pl.when(d == 1)
    def _():
        walk(range(nci), b, True)


def _gla_scan(proj, wgu, bg, onorm, *, t):
    c = SCAN_CHUNK
    tb = min(t, 512)
    nb = t // tb
    nci = tb // c
    wcum, mask, sel = _gla_consts(c)
    nlev = len(_gla_levels(c))
    qoff = 0
    koff = GLA_HEADS * GLA_DK // GLA_DK
    voff = 2 * GLA_HEADS * GLA_DK // GLA_DV
    goff = voff + GLA_HEADS
    lroff = (2 * GLA_HEADS * GLA_DK + 2 * GLA_HEADS * GLA_DV) // V7X_LANES

    def tblk(d, b):
        return d * b + (1 - d) * (nb - 1 - b)

    return pl.pallas_call(
        functools.partial(_gla_kernel, c=c, nci=nci, tb=tb),
        grid=(GLA_HEADS, 2, nb),
        in_specs=[
            pl.BlockSpec((tb, GLA_DK), lambda h, d, b: (tblk(d, b), qoff + h)),
            pl.BlockSpec((tb, GLA_DK), lambda h, d, b: (tblk(d, b), koff + h)),
            pl.BlockSpec((tb, GLA_DV), lambda h, d, b: (tblk(d, b), voff + h)),
            pl.BlockSpec((tb, GLA_DV), lambda h, d, b: (tblk(d, b), goff + h)),
            pl.BlockSpec((tb, V7X_LANES), lambda h, d, b: (tblk(d, b), lroff)),
            pl.BlockSpec((None, None, V7X_LANES, GLA_DK), lambda h, d, b: (d, h, 0, 0)),
            pl.BlockSpec((None, None, 1, GLA_DK), lambda h, d, b: (d, h, 0, 0)),
            pl.BlockSpec((None, wcum.shape[1], c), lambda h, d, b: (d, 0, 0)),
            pl.BlockSpec((None, nlev + 1, c, c), lambda h, d, b: (d, 0, 0, 0)),
            pl.BlockSpec((DIAG, GLA_DK, c), lambda h, d, b: (0, 0, 0)),
            pl.BlockSpec((1, GLA_DV), lambda h, d, b: (0, 0)),
        ],
        out_specs=pl.BlockSpec((tb, GLA_DV), lambda h, d, b: (d * b, h)),
        out_shape=jax.ShapeDtypeStruct((t, GLA_HEADS * GLA_DV), BF16),
        scratch_shapes=[pltpu.VMEM((GLA_DV, GLA_DK), F32),
                        pltpu.VMEM((t, GLA_DV), F32),
                        pltpu.VMEM((nci, c, GLA_DK), BF16),
                        pltpu.VMEM((nci, c, GLA_DV), F32),
                        pltpu.VMEM((nci, GLA_DV, GLA_DK), F32),
                        pltpu.VMEM((nci, V7X_SUBLANES, GLA_DK), F32)],
        compiler_params=_cparams("arbitrary", "arbitrary", "arbitrary"),
        name="gla_scan",
    )(proj, proj, proj, proj, proj, wgu, bg,
      jnp.asarray(wcum, BF16), jnp.asarray(mask, F32), jnp.asarray(sel, BF16),
      onorm.reshape(1, GLA_DV))


def _gla_gate_weights(w_gate_up, b_gate):
    w = w_gate_up.reshape(2, GLA_RANK, GLA_HEADS, GLA_DK).transpose(0, 2, 1, 3)
    z = jnp.zeros((GLA_HEADS, V7X_LANES, GLA_DK), F32)
    fwd = z.at[:, 0:GLA_RANK].set(w[0])
    bwd = z.at[:, GLA_RANK:2 * GLA_RANK].set(w[1])
    wgu = jnp.stack([bwd, fwd]).astype(BF16)
    bg = jnp.stack([b_gate[1], b_gate[0]]).reshape(2, GLA_HEADS, 1, GLA_DK)
    return wgu, bg


def _gdn_prep_kernel(prev_ref, cur_ref, next_ref, cw_ref, o_ref, xp_ref, *, tb, nq, nqk):
    i = pl.program_id(0)
    j = pl.program_id(1)
    halo = V7X_SUBLANES
    xp_ref[0:halo, :] = jnp.where(i > 0, prev_ref[...], 0.0)
    xp_ref[halo:halo + tb, :] = cur_ref[...]
    xp_ref[halo + tb:2 * halo + tb, :] = jnp.where(i < pl.num_programs(0) - 1, next_ref[...], 0.0)
    acc = jnp.zeros(cur_ref.shape, F32)
    for w in range(CONV_W):
        acc = acc + xp_ref[pl.ds(halo - CONV_W // 2 + w, tb), :] * cw_ref[w:w + 1, :]
    y = acc * _sigmoid(acc)

    @pl.when(j < nqk)
    def _():
        qs = jnp.where(j < nq, GDN_DK ** -0.5, 1.0)
        for hh in range(cur_ref.shape[1] // GDN_DK):
            seg = y[:, hh * GDN_DK:(hh + 1) * GDN_DK]
            ss = jnp.sum(seg * seg, axis=-1, keepdims=True)
            o_ref[:, hh * GDN_DK:(hh + 1) * GDN_DK] = (seg * (lax.rsqrt(ss + EPS) * qs)).astype(o_ref.dtype)

    @pl.when(j >= nqk)
    def _():
        o_ref[...] = y.astype(o_ref.dtype)


def _gdn_prep(proj, conv_w, *, t):
    tb = min(t, 512)
    cb = 512
    hb = tb // V7X_SUBLANES
    nt = t // tb
    ncols = conv_w.shape[1]
    nq = GDN_QK_HEADS * GDN_DK // cb
    return pl.pallas_call(
        functools.partial(_gdn_prep_kernel, tb=tb, nq=nq, nqk=2 * nq),
        grid=(nt, ncols // cb),
        in_specs=[
            pl.BlockSpec((V7X_SUBLANES, cb), lambda i, j: (jnp.maximum(i * hb - 1, 0), j)),
            pl.BlockSpec((tb, cb), lambda i, j: (i, j)),
            pl.BlockSpec((V7X_SUBLANES, cb), lambda i, j: (jnp.minimum((i + 1) * hb, nt * hb - 1), j)),
            pl.BlockSpec((CONV_W, cb), lambda i, j: (0, j)),
        ],
        out_specs=pl.BlockSpec((tb, cb), lambda i, j: (i, j)),
        out_shape=jax.ShapeDtypeStruct((t, ncols), BF16),
        scratch_shapes=[pltpu.VMEM((tb + 2 * V7X_SUBLANES, cb), F32)],
        compiler_params=_cparams("parallel", "parallel"),
        name="gdn_prep",
    )(proj, proj, proj, conv_w)


def _gdn_consts(c):
    idx = np.arange(c)
    i, m = idx[:, None], idx[None, :]
    u_f = (i <= m)
    uv_f = np.concatenate([u_f, ~u_f], axis=1).astype(np.float32)
    u_b = (i >= m)
    uv_b = np.concatenate([u_b, ~u_b], axis=1).astype(np.float32)
    uv = np.stack([uv_b, uv_f])

    def pair(b):
        return (i // (2 * b) == m // (2 * b)) & (i // b != m // b)

    lower, strict = (m <= i), (m < i)
    blk = (i // 16 == m // 16)
    levels = []
    b = 16
    while 2 * b <= c:
        levels.append(pair(b))
        b *= 2
    fwd = [lower, strict, blk & strict] + [p & strict for p in levels]
    mask_f = np.stack([x.astype(np.float32) for x in fwd])
    mask_b = np.stack([x.astype(np.float32)[::-1, ::-1] for x in fwd])
    mask = np.stack([mask_b, mask_f])
    return uv, mask, np.eye(c, dtype=np.float32), len(levels)


def _gdn_kernel(q_ref, k_ref, v_ref, z_ref, gt_ref, alog_ref, dtb_ref, uv_ref, mask_ref,
                eye_ref, onorm_ref, o_ref, s_ref, obwd_ref, mq_ref, bo_ref, et_ref,
                *, c, nci, tb, nlev):
    d = pl.program_id(1)
    b = pl.program_id(2)
    nb = pl.num_programs(2)

    @pl.when(b == 0)
    def _():
        s_ref[...] = jnp.zeros_like(s_ref)

    row8 = lax.broadcasted_iota(jnp.int32, (V7X_SUBLANES, c), 0)

    def prepare():
        chunks = range(nci)
        units = [(ci, hh) for ci in chunks for hh in range(2)]
        rows = [slice(ci * c, (ci + 1) * c) for ci in chunks]
        gts = [gt_ref[ci] for ci in chunks]
        beta8 = [_sigmoid(gt) for gt in gts]
        g8 = [-jnp.exp(alog_ref[...]) * _softplus(gt + dtb_ref[...]) for gt in gts]
        gp = [jnp.concatenate(_split3(g), axis=0) for g in g8]
        qb = [q_ref[r, :] for r in rows]
        kb = [k_ref[r, :] for r in rows]
        qf = [q.astype(F32) for q in qb]
        kf = [k.astype(F32) for k in kb]
        cl = [_dot(g, uv_ref[...]) for g in gp]
        kk = [_dot_nt(k, k) for k in kb]
        qk = [_dot_nt(q, k) for q, k in zip(qb, kb)]
        kt = [k.T for k in kf]
        cl = [x[0:8] + x[8:16] + x[16:24] for x in cl]
        cum8 = [x[:, :c] for x in cl]
        rest8 = [x[:, c:] for x in cl]
        cols = [jnp.concatenate([jnp.where(row8 < 2, bt, cm), jnp.zeros((c - V7X_SUBLANES, c), F32)],
                                axis=0).T for bt, cm in zip(beta8, cum8)]
        for ci in chunks:
            et_ref[ci] = jnp.exp(cum8[ci] + rest8[ci])

        beta_c = [cols[ci][:, hh:hh + 1] for ci, hh in units]
        cum_c = [cols[ci][:, 2 + hh:3 + hh] for ci, hh in units]
        dec = [jnp.exp(jnp.minimum(cc - cum8[ci][2 + hh:3 + hh, :], 0.0))
               for cc, (ci, hh) in zip(cum_c, units)]
        lm = [kk[ci] * dc * bc * mask_ref[1] for dc, bc, (ci, hh) in zip(dec, beta_c, units)]
        at = [(qk[ci] * dc * mask_ref[0]).astype(BF16) for dc, (ci, hh) in zip(dec, units)]

        n1 = [-(l * mask_ref[2]) for l in lm]
        x = [eye_ref[...] + n for n in n1]
        p = [n.astype(BF16) for n in n1]
        for _ in range(3):
            p = [_dot(pp, pp).astype(BF16) for pp in p]
            x = [xx + _dot(xx.astype(BF16), pp) for xx, pp in zip(x, p)]
        for lv in range(nlev):
            xb = [xx.astype(BF16) for xx in x]
            t1 = [_dot((l * mask_ref[3 + lv]).astype(BF16), b).astype(BF16) for l, b in zip(lm, xb)]
            x = [xx - _dot(b, t) for xx, b, t in zip(x, xb, t1)]

        e_c = [jnp.exp(cc) for cc in cum_c]
        rhs = [jnp.concatenate([v_ref[rows[ci], hh * GDN_DV:(hh + 1) * GDN_DV].astype(F32) * bc,
                                kf[ci] * (bc * ec)], axis=1).astype(BF16)
               for bc, ec, (ci, hh) in zip(beta_c, e_c, units)]
        uw = [_dot(xx.astype(BF16), r).astype(BF16) for xx, r in zip(x, rhs)]
        kdt = [(kt[ci] * jnp.exp(rest8[ci][2 + hh:3 + hh, :])).astype(BF16) for ci, hh in units]
        pr = [_dot(jnp.concatenate([kd, a], axis=0), w) for kd, a, w in zip(kdt, at, uw)]
        for n, (ci, hh) in enumerate(units):
            m = -pr[n][:GDN_DK, GDN_DV:]
            qm = qf[ci] * e_c[n] - pr[n][GDN_DK:, GDN_DV:]
            mq_ref[ci, hh] = jnp.concatenate([m, qm], axis=0).astype(BF16)
            bo_ref[ci, hh] = pr[n][:, :GDN_DV]

    def step(ci, blk, forward):
        rows = slice(ci * c, (ci + 1) * c)
        et = et_ref[ci]
        for hh in range(2):
            s = s_ref[hh]
            r = _dot(mq_ref[ci, hh], s.astype(BF16)) + bo_ref[ci, hh]
            s_ref[hh] = s * et[2 + hh:3 + hh, :] + r[:GDN_DK]
            o = r[GDN_DK:]
            orow = pl.ds(pl.multiple_of(blk * tb + ci * c, c), c)
            ocol = slice(hh * GDN_DV, (hh + 1) * GDN_DV)
            if forward:
                ot = o + obwd_ref[orow, ocol]
                y = ot * _rms_scale(ot) * onorm_ref[...]
                z = z_ref[rows, ocol]
                o_ref[rows, ocol] = (y * (z * _sigmoid(z))).astype(o_ref.dtype)
            else:
                obwd_ref[orow, ocol] = o

    prepare()

    @pl.when(d == 0)
    def _():
        for i in range(nci):
            step(nci - 1 - i, nb - 1 - b, False)

    @pl.when(d == 1)
    def _():
        for i in range(nci):
            step(i, b, True)


def _gdn_scan(qkv, proj, gt, alog, dtb, onorm, *, t):
    c = SCAN_CHUNK
    tb = min(t, 1024)
    nb = t // tb
    nci = tb // c
    uv, mask, eye, nlev = _gdn_consts(c)
    hw = 2 * GDN_DV
    voff = 2 * GDN_QK_HEADS * GDN_DK // hw
    zoff = (2 * GDN_QK_HEADS * GDN_DK + 2 * GDN_QK_HEADS * GDN_DV) // hw

    def tblk(d, b):
        return d * b + (1 - d) * (nb - 1 - b)

    return pl.pallas_call(
        functools.partial(_gdn_kernel, c=c, nci=nci, tb=tb, nlev=nlev),
        grid=(GDN_QK_HEADS, 2, nb),
        in_specs=[
            pl.BlockSpec((tb, GDN_DK), lambda h, d, b: (tblk(d, b), h)),
            pl.BlockSpec((tb, GDN_DK), lambda h, d, b: (tblk(d, b), GDN_QK_HEADS + h)),
            pl.BlockSpec((tb, hw), lambda h, d, b: (tblk(d, b), voff + h)),
            pl.BlockSpec((tb, hw), lambda h, d, b: (tblk(d, b), zoff + h)),
            pl.BlockSpec((None, None, nci, V7X_SUBLANES, c), lambda h, d, b: (d, h, tblk(d, b), 0, 0)),
            pl.BlockSpec((None, None, V7X_SUBLANES, c), lambda h, d, b: (d, h, 0, 0)),
            pl.BlockSpec((None, None, V7X_SUBLANES, c), lambda h, d, b: (d, h, 0, 0)),
            pl.BlockSpec((None, c, 2 * c), lambda h, d, b: (d, 0, 0)),
            pl.BlockSpec((None, mask.shape[1], c, c), lambda h, d, b: (d, 0, 0, 0)),
            pl.BlockSpec((c, c), lambda h, d, b: (0, 0)),
            pl.BlockSpec((1, GDN_DV), lambda h, d, b: (0, 0)),
        ],
        out_specs=pl.BlockSpec((tb, hw), lambda h, d, b: (d * b, h)),
        out_shape=jax.ShapeDtypeStruct((t, 2 * GDN_QK_HEADS * GDN_DV), BF16),
        scratch_shapes=[pltpu.VMEM((2, GDN_DK, GDN_DV), F32),
                        pltpu.VMEM((t, hw), F32),
                        pltpu.VMEM((nci, 2, GDN_DK + c, GDN_DK), BF16),
                        pltpu.VMEM((nci, 2, GDN_DK + c, GDN_DV), F32),
                        pltpu.VMEM((nci, V7X_SUBLANES, c), F32)],
        compiler_params=_cparams("arbitrary", "arbitrary", "arbitrary"),
        name="gdn_scan",
    )(qkv, qkv, qkv, proj, gt, alog, dtb,
      jnp.asarray(uv, BF16), jnp.asarray(mask, F32), jnp.asarray(eye, F32),
      onorm.reshape(1, GDN_DV))


def _gdn_gate_layout(ba, a_log, dt_bias, *, t):
    c = SCAN_CHUNK
    g = ba.reshape(t, 2, 2, GDN_QK_HEADS, 2)
    g = g.transpose(2, 3, 1, 4, 0).reshape(2, GDN_QK_HEADS, 4, t)
    g = jnp.concatenate([g, jnp.zeros_like(g)], axis=2)
    g = g.reshape(2, GDN_QK_HEADS, V7X_SUBLANES, t // c, c).transpose(0, 1, 3, 2, 4)
    g = jnp.stack([g[1], g[0]])

    def rows(pv):
        pv = pv.reshape(2, GDN_QK_HEADS, 2)
        z = jnp.zeros_like(pv)
        r = jnp.concatenate([z, pv, z, z], axis=2)
        r = jnp.broadcast_to(r[..., None], (2, GDN_QK_HEADS, V7X_SUBLANES, c))
        return jnp.stack([r[1], r[0]])

    return g, rows(a_log.astype(F32)), rows(dt_bias.astype(F32))


def _pad_cols(w, n):
    return jnp.concatenate([w, jnp.zeros((w.shape[0], n - w.shape[1]), w.dtype)], axis=1)


def _proj_tile(n):
    best = None
    for tn in range(PROJ_TN_MAX, 3 * V7X_MXU_COLS, -V7X_MXU_COLS):
        n_pad = -(-n // tn) * tn
        if best is None or n_pad < best[1]:
            best = (tn, n_pad)
    return best


def kernel(x, p, mixer_norm, gla_w_in, gla_w_gate_up, gla_b_gate, gla_out_norm, gla_w_out,
           gdn_w_in, gdn_conv, gdn_a_log, gdn_dt_bias, gdn_out_norm, gdn_w_out,
           ffn_norm, ffn_w_in, ffn_w_out, ple_norm, ple_w_gate, ple_w_proj, final_norm):
    _, t, dm = x.shape
    depth = mixer_norm.shape[0]
    h = x.reshape(t, dm)
    tm = min(t, 1024)
    tm_ple = min(t, 512)
    tn_out = 1024
    tf = 512
    for i in range(depth):
        j = i // 2
        if i % 2 == 0:
            tn, n_pad = _proj_tile(gla_w_in.shape[2])
            w_in = _pad_cols(gla_w_in[j].astype(BF16), n_pad)
            proj = _norm_matmul(h, mixer_norm[i], w_in, tm=tm, tn=tn)
            wgu, bg = _gla_gate_weights(gla_w_gate_up[j], gla_b_gate[j])
            o = _gla_scan(proj, wgu, bg, gla_out_norm[j], t=t)
            h = _matmul_res(o, gla_w_out[j].astype(BF16), h, tm=tm, tn=tn_out)
        else:
            tn, n_pad = _proj_tile(gdn_w_in.shape[2])
            w_in = _pad_cols(gdn_w_in[j].astype(BF16), n_pad)
            proj = _norm_matmul(h, mixer_norm[i], w_in, tm=tm, tn=tn)
            kd = GDN_QK_HEADS * GDN_DK
            vd = 2 * GDN_QK_HEADS * GDN_DV
            qkv = _gdn_prep(proj, gdn_conv[j], t=t)
            ba = proj[:, 2 * kd + 2 * vd:2 * kd + 2 * vd + 4 * 2 * GDN_QK_HEADS]
            gt, alog, dtb = _gdn_gate_layout(ba, gdn_a_log[j], gdn_dt_bias[j], t=t)
            o = _gdn_scan(qkv, proj, gt, alog, dtb, gdn_out_norm[j], t=t)
            h = _matmul_res(o, gdn_w_out[j].astype(BF16), h, tm=tm, tn=tn_out)
        h = _ffn(h, ffn_norm[i], ffn_w_in[i].astype(BF16), ffn_w_out[i].astype(BF16), tm=tm, tf=tf)
        h = _ple(h, ple_norm[i], ple_w_gate[i].astype(BF16), p[i].reshape(t, -1),
                 ple_w_proj[i].astype(BF16), final_norm, tm=tm_ple, final=(i == depth - 1))
    return h.reshape(x.shape)
```

```python
import functools

import numpy as np
import jax
import jax.numpy as jnp
from jax import lax
from jax.experimental import pallas as pl
from jax.experimental.pallas import tpu as pltpu

F32 = jnp.float32
BF16 = jnp.bfloat16
EPS = 1e-6

V7X_LANES = 128
V7X_SUBLANES = 8
V7X_VMEM_LIMIT_BYTES = 56 * 1024 * 1024

GLA_HEADS = 4
GLA_DK = 256
GLA_DV = 512
GLA_RANK = 16
GLA_GATE_NORM = 16.0
GDN_QK_HEADS = 16
GDN_DK = 128
GDN_DV = 128
CONV_W = 5
SCAN_CHUNK = 128
LOG2_E = 1.4426950408889634
UNIT_CHUNKS = 8
PREP_ROWS = 64
V7X_MXU_COLS = 256
PROJ_TN_MAX = 1792


def _cparams(*sem):
    return pltpu.CompilerParams(dimension_semantics=sem,
                                vmem_limit_bytes=V7X_VMEM_LIMIT_BYTES)


def _dot(a, b):
    return jnp.dot(a, b, preferred_element_type=F32)


def _dot_nt(a, b):
    return lax.dot_general(a, b, (((1,), (1,)), ((), ())), preferred_element_type=F32)


def _dot_tn(a, b):
    return lax.dot_general(a, b, (((0,), (0,)), ((), ())), preferred_element_type=F32)


def _rms_scale(x):
    return lax.rsqrt(jnp.mean(x * x, axis=-1, keepdims=True) + EPS)


def _sigmoid(x):
    return 1.0 / (1.0 + jnp.exp(-x))


def _softplus(x):
    return jnp.maximum(x, 0.0) + jnp.log(1.0 + jnp.exp(-jnp.abs(x)))


def _split3(x):
    p1 = x.astype(BF16)
    r = x - p1.astype(F32)
    p2 = r.astype(BF16)
    p3 = (r - p2.astype(F32)).astype(BF16)
    return p1, p2, p3


def _norm_matmul_kernel(x_ref, nw_ref, w_ref, o_ref, xn_ref):
    @pl.when(pl.program_id(1) == 0)
    def _():
        x = x_ref[...]
        xn_ref[...] = (x * _rms_scale(x) * nw_ref[...]).astype(BF16)

    o_ref[...] = _dot(xn_ref[...], w_ref[...]).astype(o_ref.dtype)


def _norm_matmul(x, nw, w, *, tm, tn):
    t, d = x.shape
    n = w.shape[1]
    return pl.pallas_call(
        _norm_matmul_kernel,
        grid=(t // tm, n // tn),
        in_specs=[pl.BlockSpec((tm, d), lambda i, j: (i, 0)),
                  pl.BlockSpec((1, d), lambda i, j: (0, 0)),
                  pl.BlockSpec((d, tn), lambda i, j: (0, j))],
        out_specs=pl.BlockSpec((tm, tn), lambda i, j: (i, j)),
        out_shape=jax.ShapeDtypeStruct((t, n), F32),
        scratch_shapes=[pltpu.VMEM((tm, d), BF16)],
        compiler_params=_cparams("parallel", "arbitrary"),
        name="norm_matmul",
    )(x, nw.reshape(1, d), w)


def _matmul_res_kernel(a_ref, w_ref, h_ref, o_ref):
    o_ref[...] = h_ref[...] + _dot(a_ref[...], w_ref[...])


def _matmul_res(a, w, h, *, tm, tn):
    t, k = a.shape
    n = w.shape[1]
    return pl.pallas_call(
        _matmul_res_kernel,
        grid=(t // tm, n // tn),
        in_specs=[pl.BlockSpec((tm, k), lambda i, j: (i, 0)),
                  pl.BlockSpec((k, tn), lambda i, j: (0, j)),
                  pl.BlockSpec((tm, tn), lambda i, j: (i, j))],
        out_specs=pl.BlockSpec((tm, tn), lambda i, j: (i, j)),
        out_shape=jax.ShapeDtypeStruct((t, n), F32),
        compiler_params=_cparams("parallel", "arbitrary"),
        name="matmul_res",
    )(a, w, h)


def _ffn_kernel(x_ref, nw_ref, wg_ref, wu_ref, wo_ref, o_ref, xn_ref):
    @pl.when(pl.program_id(1) == 0)
    def _():
        x = x_ref[...]
        xn_ref[...] = (x * _rms_scale(x) * nw_ref[...]).astype(BF16)
        o_ref[...] = x

    xn = xn_ref[...]
    g = _dot(xn, wg_ref[...])
    u = _dot(xn, wu_ref[...])
    act = (g * _sigmoid(g) * u).astype(BF16)
    o_ref[...] += _dot(act, wo_ref[...])


def _ffn(x, nw, w_in, w_out, *, tm, tf):
    t, d = x.shape
    f = w_out.shape[0]
    nf = f // tf
    return pl.pallas_call(
        _ffn_kernel,
        grid=(t // tm, nf),
        in_specs=[pl.BlockSpec((tm, d), lambda i, j: (i, 0), pipeline_mode=pl.Buffered(1)),
                  pl.BlockSpec((1, d), lambda i, j: (0, 0)),
                  pl.BlockSpec((d, tf), lambda i, j: (0, j)),
                  pl.BlockSpec((d, tf), lambda i, j: (0, nf + j)),
                  pl.BlockSpec((tf, d), lambda i, j: (j, 0))],
        out_specs=pl.BlockSpec((tm, d), lambda i, j: (i, 0)),
        out_shape=jax.ShapeDtypeStruct((t, d), F32),
        scratch_shapes=[pltpu.VMEM((tm, d), BF16)],
        compiler_params=_cparams("parallel", "arbitrary"),
        name="ffn",
    )(x, nw.reshape(1, d), w_in, w_in, w_out)


def _ple_kernel(h_ref, nw_ref, wg_ref, p_ref, wp_ref, fn_ref, o_ref, *, final):
    h = h_ref[...]
    xn = (h * _rms_scale(h) * nw_ref[...]).astype(BF16)
    gate = _sigmoid(_dot(xn, wg_ref[...]))
    y = h + gate * _dot(p_ref[...].astype(BF16), wp_ref[...])
    if final:
        y = y * _rms_scale(y) * fn_ref[...]
    o_ref[...] = y


def _ple(h, nw, wg, p, wp, fn, *, tm, final):
    t, d = h.shape
    kp = p.shape[1]
    return pl.pallas_call(
        functools.partial(_ple_kernel, final=final),
        grid=(t // tm,),
        in_specs=[pl.BlockSpec((tm, d), lambda i: (i, 0)),
                  pl.BlockSpec((1, d), lambda i: (0, 0)),
                  pl.BlockSpec((d, d), lambda i: (0, 0), pipeline_mode=pl.Buffered(1)),
                  pl.BlockSpec((tm, kp), lambda i: (i, 0)),
                  pl.BlockSpec((kp, d), lambda i: (0, 0), pipeline_mode=pl.Buffered(1)),
                  pl.BlockSpec((1, d), lambda i: (0, 0))],
        out_specs=pl.BlockSpec((tm, d), lambda i: (i, 0)),
        out_shape=jax.ShapeDtypeStruct((t, d), F32),
        compiler_params=_cparams("parallel"),
        name="ple",
    )(h, nw.reshape(1, d), wg, p, wp, fn.reshape(1, d))


def _gla_levels(c):
    out, b = [], c // 2
    while b >= 1:
        out.append(b)
        b //= 2
    return out


def _gla_consts(c):
    idx = np.arange(c)
    i, m = idx[:, None], idx[None, :]
    blocks = [(m <= i)]
    masks = []
    for b in _gla_levels(c):
        mid = (idx // (2 * b)) * (2 * b) + b - 1
        md = mid[:, None]
        blocks.append(np.where(i > md, (m > md) & (m <= i), (m > i) & (m <= md)))
        masks.append((i // (2 * b) == m // (2 * b)) & (i % (2 * b) >= b) & (m % (2 * b) < b))
    masks.append(i == m)
    fwd_w = np.concatenate([bl.astype(np.float32) for bl in blocks], axis=0)
    bwd_w = np.concatenate([bl.astype(np.float32)[::-1, ::-1] for bl in blocks], axis=0)
    ones = np.ones((V7X_SUBLANES, c), np.float32)
    wcum = np.stack([np.concatenate([bwd_w, ones], 0), np.concatenate([fwd_w, ones], 0)])
    fwd_m = np.stack([mk.astype(np.float32) for mk in masks])
    bwd_m = np.stack([mk.astype(np.float32)[::-1, ::-1] for mk in masks])
    mask = np.stack([bwd_m, fwd_m])
    return wcum, mask


def _gla_kernel(q_ref, k_ref, v_ref, og_ref, lr_ref, wgu_ref, bg_ref, wcum_ref, mask_ref,
                onorm_ref, o_ref, st_ref, obwd_ref, qd_ref, oa_ref, dl_ref, et_ref,
                *, c, nci, tb):
    d = pl.program_id(1)
    b = pl.program_id(2)
    nb = pl.num_programs(2)
    nlev = len(_gla_levels(c))

    @pl.when(b == 0)
    def _():
        st_ref[...] = jnp.zeros_like(st_ref)

    def prepare():
        chunks = range(nci)
        rows = [slice(ci * c, (ci + 1) * c) for ci in chunks]
        q = [q_ref[r, :] * (GLA_DK ** -0.5) for r in rows]
        k = [k_ref[r, :] for r in rows]
        vb = [v_ref[r, :].astype(BF16) for r in rows]
        gk = [_dot(lr_ref[r, :].astype(BF16), wgu_ref[...]) + bg_ref[...] for r in rows]
        g = [(jnp.minimum(x, 0.0) - jnp.log(1.0 + jnp.exp(-jnp.abs(x)))) * (LOG2_E / GLA_GATE_NORM) for x in gk]
        gp = [jnp.concatenate(_split3(x)[:2], axis=1) for x in g]
        cums = [_dot(wcum_ref[...], x) for x in gp]
        cums = [x[:, :GLA_DK] + x[:, GLA_DK:] for x in cums]
        cum = [x[0:c] for x in cums]
        tot = [x[(1 + nlev) * c:(1 + nlev) * c + V7X_SUBLANES] for x in cums]
        rest = [tt[0:1] - cc for tt, cc in zip(tot, cum)]
        for ci in chunks:
            et_ref[ci] = jnp.exp2(tot[ci])

        a = [jnp.sum(qq * kk, axis=-1, keepdims=True) * mask_ref[nlev] for qq, kk in zip(q, k)]
        for lv in range(nlev):
            e = [jnp.exp2(x[(1 + lv) * c:(2 + lv) * c]) for x in cums]
            s = [_dot_nt((qq * ee).astype(BF16), (kk * ee).astype(BF16)) for qq, kk, ee in zip(q, k, e)]
            a = [aa + ss * mask_ref[lv] for aa, ss in zip(a, s)]
        a = [aa.astype(BF16) for aa in a]

        kd = [(kk * jnp.exp2(rr)).astype(BF16) for kk, rr in zip(k, rest)]
        for ci in chunks:
            qd_ref[ci] = (q[ci] * jnp.exp2(cum[ci])).astype(BF16)
            oa_ref[ci] = _dot(a[ci], vb[ci])
            dl_ref[ci] = _dot_tn(vb[ci], kd[ci])

    def walk(order, blk, forward):
        for ci in order:
            rows = slice(ci * c, (ci + 1) * c)
            st = st_ref[...]
            o = _dot_nt(qd_ref[ci], st.astype(BF16)) + oa_ref[ci]
            st_ref[...] = st * et_ref[ci][0:1] + dl_ref[ci]
            orow = pl.ds(pl.multiple_of(blk * tb + ci * c, c), c)
            if forward:
                ot = o + obwd_ref[orow, :]
                y = ot * _rms_scale(ot) * onorm_ref[...]
                og = og_ref[rows, :]
                o_ref[rows, :] = (y * (og * _sigmoid(og))).astype(o_ref.dtype)
            else:
                obwd_ref[orow, :] = o

    prepare()

    @pl.when(d == 0)
    def _():
        walk(range(nci - 1, -1, -1), nb - 1 - b, False)

    @pl.when(d == 1)
    def _():
        walk(range(nci), b, True)


def _gla_scan(proj, wgu, bg, onorm, *, t):
    c = SCAN_CHUNK
    tb = min(t, 512)
    nb = t // tb
    nci = tb // c
    wcum, mask = _gla_consts(c)
    nlev = len(_gla_levels(c))
    qoff = 0
    koff = GLA_HEADS * GLA_DK // GLA_DK
    voff = 2 * GLA_HEADS * GLA_DK // GLA_DV
    goff = voff + GLA_HEADS
    lroff = (2 * GLA_HEADS * GLA_DK + 2 * GLA_HEADS * GLA_DV) // V7X_LANES

    def tblk(d, b):
        return d * b + (1 - d) * (nb - 1 - b)

    return pl.pallas_call(
        functools.partial(_gla_kernel, c=c, nci=nci, tb=tb),
        grid=(GLA_HEADS, 2, nb),
        in_specs=[
            pl.BlockSpec((tb, GLA_DK), lambda h, d, b: (tblk(d, b), qoff + h)),
            pl.BlockSpec((tb, GLA_DK), lambda h, d, b: (tblk(d, b), koff + h)),
            pl.BlockSpec((tb, GLA_DV), lambda h, d, b: (tblk(d, b), voff + h)),
            pl.BlockSpec((tb, GLA_DV), lambda h, d, b: (tblk(d, b), goff + h)),
            pl.BlockSpec((tb, V7X_LANES), lambda h, d, b: (tblk(d, b), lroff)),
            pl.BlockSpec((None, None, V7X_LANES, GLA_DK), lambda h, d, b: (d, h, 0, 0)),
            pl.BlockSpec((None, None, 1, GLA_DK), lambda h, d, b: (d, h, 0, 0)),
            pl.BlockSpec((None, wcum.shape[1], c), lambda h, d, b: (d, 0, 0)),
            pl.BlockSpec((None, nlev + 1, c, c), lambda h, d, b: (d, 0, 0, 0)),
            pl.BlockSpec((1, GLA_DV), lambda h, d, b: (0, 0)),
        ],
        out_specs=pl.BlockSpec((tb, GLA_DV), lambda h, d, b: (d * b, h)),
        out_shape=jax.ShapeDtypeStruct((t, GLA_HEADS * GLA_DV), BF16),
        scratch_shapes=[pltpu.VMEM((GLA_DV, GLA_DK), F32),
                        pltpu.VMEM((t, GLA_DV), F32),
                        pltpu.VMEM((nci, c, GLA_DK), BF16),
                        pltpu.VMEM((nci, c, GLA_DV), F32),
                        pltpu.VMEM((nci, GLA_DV, GLA_DK), F32),
                        pltpu.VMEM((nci, V7X_SUBLANES, GLA_DK), F32)],
        compiler_params=_cparams("arbitrary", "arbitrary", "arbitrary"),
        name="gla_scan",
    )(proj, proj, proj, proj, proj, wgu, bg,
      jnp.asarray(wcum, BF16), jnp.asarray(mask, F32),
      onorm.reshape(1, GLA_DV))


def _gla_gate_weights(w_gate_up, b_gate):
    w = w_gate_up.reshape(2, GLA_RANK, GLA_HEADS, GLA_DK).transpose(0, 2, 1, 3)
    z = jnp.zeros((GLA_HEADS, V7X_LANES, GLA_DK), F32)
    fwd = z.at[:, 0:GLA_RANK].set(w[0])
    bwd = z.at[:, GLA_RANK:2 * GLA_RANK].set(w[1])
    wgu = jnp.stack([bwd, fwd]).astype(BF16)
    bg = jnp.stack([b_gate[1], b_gate[0]]).reshape(2, GLA_HEADS, 1, GLA_DK)
    return wgu, bg


def _gdn_prep_kernel(prev_ref, cur_ref, next_ref, cw_ref, o_ref, xp_ref, *, tb, nq, nqk):
    i = pl.program_id(0)
    j = pl.program_id(1)
    halo = V7X_SUBLANES
    xp_ref[0:halo, :] = jnp.where(i > 0, prev_ref[...], 0.0)
    xp_ref[halo:halo + tb, :] = cur_ref[...]
    xp_ref[halo + tb:2 * halo + tb, :] = jnp.where(i < pl.num_programs(0) - 1, next_ref[...], 0.0)
    is_qk = j < nqk
    qs = jnp.where(j < nq, GDN_DK ** -0.5, 1.0)
    n = PREP_ROWS + 2 * halo
    for r0 in range(0, tb, PREP_ROWS):
        for hh in range(cur_ref.shape[1] // GDN_DK):
            cols = slice(hh * GDN_DK, (hh + 1) * GDN_DK)
            xs = xp_ref[r0:r0 + n, cols]
            acc = jnp.zeros((PREP_ROWS, GDN_DK), F32)
            for w in range(CONV_W):
                sh = xs if w == CONV_W // 2 else pltpu.roll(xs, (CONV_W // 2 - w) % n, axis=0)
                acc = acc + sh[halo:halo + PREP_ROWS] * cw_ref[w:w + 1, cols]
            y = acc * _sigmoid(acc)
            ss = jnp.sum(y * y, axis=-1, keepdims=True)
            scale = jnp.where(is_qk, lax.rsqrt(ss + EPS) * qs, 1.0)
            o_ref[r0:r0 + PREP_ROWS, cols] = (y * scale).astype(o_ref.dtype)


def _gdn_prep(proj, conv_w, *, t):
    tb = min(t, 512)
    cb = 512
    hb = tb // V7X_SUBLANES
    nt = t // tb
    ncols = conv_w.shape[1]
    nq = GDN_QK_HEADS * GDN_DK // cb
    return pl.pallas_call(
        functools.partial(_gdn_prep_kernel, tb=tb, nq=nq, nqk=2 * nq),
        grid=(nt, ncols // cb),
        in_specs=[
            pl.BlockSpec((V7X_SUBLANES, cb), lambda i, j: (jnp.maximum(i * hb - 1, 0), j)),
            pl.BlockSpec((tb, cb), lambda i, j: (i, j)),
            pl.BlockSpec((V7X_SUBLANES, cb), lambda i, j: (jnp.minimum((i + 1) * hb, nt * hb - 1), j)),
            pl.BlockSpec((CONV_W, cb), lambda i, j: (0, j)),
        ],
        out_specs=pl.BlockSpec((tb, cb), lambda i, j: (i, j)),
        out_shape=jax.ShapeDtypeStruct((t, ncols), BF16),
        scratch_shapes=[pltpu.VMEM((tb + 2 * V7X_SUBLANES, cb), F32)],
        compiler_params=_cparams("parallel", "parallel"),
        name="gdn_prep",
    )(proj, proj, proj, conv_w)


def _gdn_consts(c):
    idx = np.arange(c)
    i, m = idx[:, None], idx[None, :]
    u_f = (i <= m)
    uv_f = np.concatenate([u_f, ~u_f], axis=1).astype(np.float32)
    u_b = (i >= m)
    uv_b = np.concatenate([u_b, ~u_b], axis=1).astype(np.float32)
    uv = np.stack([uv_b, uv_f])

    def pair(b):
        return (i // (2 * b) == m // (2 * b)) & (i // b != m // b)

    lower, strict = (m <= i), (m < i)
    blk = (i // 16 == m // 16)
    levels = []
    b = 16
    while 2 * b <= c:
        levels.append(pair(b))
        b *= 2
    fwd = [lower, strict, blk & strict] + [p & strict for p in levels]
    mask_f = np.stack([x.astype(np.float32) for x in fwd])
    mask_b = np.stack([x.astype(np.float32)[::-1, ::-1] for x in fwd])
    mask = np.stack([mask_b, mask_f])
    return uv, mask, np.eye(c, dtype=np.float32), len(levels)


def _gdn_kernel(q_ref, k_ref, v_ref, z_ref, gt_ref, alog_ref, dtb_ref, uv_ref, mask_ref,
                eye_ref, onorm_ref, o_ref, s_ref, obwd_ref, mq_ref, bo_ref, et_ref,
                *, c, nci, tb, nb, nlev):
    hq = pl.program_id(0)
    d = pl.program_id(1)
    b = pl.program_id(2)

    @pl.when((hq == 0) & (d == 0) & (b == 0))
    def _():
        obwd_ref[...] = jnp.zeros_like(obwd_ref)
        s_ref[...] = jnp.zeros_like(s_ref)
        mq_ref[...] = jnp.zeros_like(mq_ref)
        bo_ref[...] = jnp.zeros_like(bo_ref)
        et_ref[...] = jnp.zeros_like(et_ref)

    row8 = lax.broadcasted_iota(jnp.int32, (V7X_SUBLANES, c), 0)

    def prepare(slot, chunk_ids):
        chunks = range(len(chunk_ids))
        units = [(ci, hh) for ci in chunks for hh in range(2)]
        rows = [slice(ca * c, (ca + 1) * c) for ca in chunk_ids]
        gts = [gt_ref[ca] for ca in chunk_ids]
        beta8 = [_sigmoid(gt) for gt in gts]
        g8 = [-jnp.exp(alog_ref[...]) * _softplus(gt + dtb_ref[...]) for gt in gts]
        gp = [jnp.concatenate(_split3(g), axis=0) for g in g8]
        qb = [q_ref[r, :] for r in rows]
        kb = [k_ref[r, :] for r in rows]
        qf = [q.astype(F32) for q in qb]
        kf = [k.astype(F32) for k in kb]
        cl = [_dot(g, uv_ref[...]) for g in gp]
        kk = [_dot_nt(k, k) for k in kb]
        qk = [_dot_nt(q, k) for q, k in zip(qb, kb)]
        kt = [k.T for k in kf]
        cl = [x[0:8] + x[8:16] + x[16:24] for x in cl]
        cum8 = [x[:, :c] for x in cl]
        rest8 = [x[:, c:] for x in cl]
        cols = [jnp.concatenate([jnp.where(row8 < 2, bt, cm), jnp.zeros((c - V7X_SUBLANES, c), F32)],
                                axis=0).T for bt, cm in zip(beta8, cum8)]
        for ci in chunks:
            et_ref[slot, chunk_ids[ci]] = jnp.exp(cum8[ci] + rest8[ci])
        yield

        beta_c = [cols[ci][:, hh:hh + 1] for ci, hh in units]
        cum_c = [cols[ci][:, 2 + hh:3 + hh] for ci, hh in units]
        dec = [jnp.exp(jnp.minimum(cc - cum8[ci][2 + hh:3 + hh, :], 0.0))
               for cc, (ci, hh) in zip(cum_c, units)]
        lm = [kk[ci] * dc * bc * mask_ref[1] for dc, bc, (ci, hh) in zip(dec, beta_c, units)]
        at = [(qk[ci] * dc * mask_ref[0]).astype(BF16) for dc, (ci, hh) in zip(dec, units)]

        n1 = [-(l * mask_ref[2]) for l in lm]
        x = [eye_ref[...] + n for n in n1]
        p = [n.astype(BF16) for n in n1]
        yield
        for _ in range(3):
            p = [_dot(pp, pp).astype(BF16) for pp in p]
            x = [xx + _dot(xx.astype(BF16), pp) for xx, pp in zip(x, p)]
            yield
        for lv in range(nlev):
            xb = [xx.astype(BF16) for xx in x]
            t1 = [_dot((l * mask_ref[3 + lv]).astype(BF16), bb).astype(BF16) for l, bb in zip(lm, xb)]
            x = [xx - _dot(bb, tt) for xx, bb, tt in zip(x, xb, t1)]
            yield

        e_c = [jnp.exp(cc) for cc in cum_c]
        rhs = [jnp.concatenate([v_ref[rows[ci], hh * GDN_DV:(hh + 1) * GDN_DV].astype(F32) * bc,
                                kf[ci] * (bc * ec)], axis=1).astype(BF16)
               for bc, ec, (ci, hh) in zip(beta_c, e_c, units)]
        uw = [_dot(xx.astype(BF16), r).astype(BF16) for xx, r in zip(x, rhs)]
        yield
        kdt = [(kt[ci] * jnp.exp(rest8[ci][2 + hh:3 + hh, :])).astype(BF16) for ci, hh in units]
        pr = [_dot(jnp.concatenate([kd, a], axis=0), w) for kd, a, w in zip(kdt, at, uw)]
        for n, (ci, hh) in enumerate(units):
            m = -pr[n][:GDN_DK, GDN_DV:]
            qm = qf[ci] * e_c[n] - pr[n][GDN_DK:, GDN_DV:]
            mq_ref[slot, chunk_ids[ci], hh] = jnp.concatenate([m, qm], axis=0).astype(BF16)
            bo_ref[slot, chunk_ids[ci], hh] = pr[n][:, :GDN_DV]

    forward = d == 1

    def walk_step(i, slot, wb):
        ci = jnp.where(forward, i, nci - 1 - i)
        r0 = pl.multiple_of(ci * c, c)
        rows = pl.ds(r0, c)
        orow = pl.ds(pl.multiple_of(wb * tb + r0, c), c)
        et = et_ref[slot, ci]
        for hh in range(2):
            s = s_ref[hh]
            r = _dot(mq_ref[slot, ci, hh], s.astype(BF16)) + bo_ref[slot, ci, hh]
            s_ref[hh] = s * et[2 + hh:3 + hh, :] + r[:GDN_DK]
            o = r[GDN_DK:]
            ocol = slice(hh * GDN_DV, (hh + 1) * GDN_DV)
            parked = obwd_ref[orow, ocol]
            ot = o + jnp.where(forward, parked, 0.0)
            obwd_ref[orow, ocol] = jnp.where(forward, parked, o)
            y = ot * _rms_scale(ot) * onorm_ref[...]
            z = z_ref[rows, ocol]
            o_ref[rows, ocol] = (y * (z * _sigmoid(z))).astype(o_ref.dtype)

    def time_block(step_b):
        return jnp.where(forward, step_b, nb - 1 - step_b)

    @pl.when(b < nb)
    def _():
        wb = time_block(jnp.maximum(b - 1, 0))
        ngrp = -(-nci // UNIT_CHUNKS)
        per = nci // ngrp
        for g in range(ngrp):
            stages = prepare(b % 2, list(range(g * per, (g + 1) * per)))
            for i in range(g * per, (g + 1) * per):
                next(stages)
                walk_step(i, (b + 1) % 2, wb)
            for _ in stages:
                pass

    @pl.when(b == 0)
    def _():
        s_ref[...] = jnp.zeros_like(s_ref)

    @pl.when(b == nb)
    def _():
        for i in range(nci):
            walk_step(i, (nb - 1) % 2, time_block(nb - 1))


def _gdn_scan(qkv, proj, gt, alog, dtb, onorm, *, t, tb):
    c = SCAN_CHUNK
    nb = t // tb
    nci = tb // c
    uv, mask, eye, nlev = _gdn_consts(c)
    hw = 2 * GDN_DV
    voff = 2 * GDN_QK_HEADS * GDN_DK // hw
    zoff = (2 * GDN_QK_HEADS * GDN_DK + 2 * GDN_QK_HEADS * GDN_DV) // hw

    def tblk(d, b):
        return d * b + (1 - d) * (nb - 1 - b)

    def prep_blk(d, b):
        return tblk(d, jnp.minimum(b, nb - 1))

    def walk_blk(d, b):
        return tblk(d, jnp.maximum(b - 1, 0))

    return pl.pallas_call(
        functools.partial(_gdn_kernel, c=c, nci=nci, tb=tb, nb=nb, nlev=nlev),
        grid=(GDN_QK_HEADS, 2, nb + 1),
        in_specs=[
            pl.BlockSpec((tb, GDN_DK), lambda h, d, b: (prep_blk(d, b), h)),
            pl.BlockSpec((tb, GDN_DK), lambda h, d, b: (prep_blk(d, b), GDN_QK_HEADS + h)),
            pl.BlockSpec((tb, hw), lambda h, d, b: (prep_blk(d, b), voff + h)),
            pl.BlockSpec((tb, hw), lambda h, d, b: (walk_blk(d, b), zoff + h)),
            pl.BlockSpec((None, None, nci, V7X_SUBLANES, c), lambda h, d, b: (d, h, prep_blk(d, b), 0, 0)),
            pl.BlockSpec((None, None, V7X_SUBLANES, c), lambda h, d, b: (d, h, 0, 0)),
            pl.BlockSpec((None, None, V7X_SUBLANES, c), lambda h, d, b: (d, h, 0, 0)),
            pl.BlockSpec((None, c, 2 * c), lambda h, d, b: (d, 0, 0)),
            pl.BlockSpec((None, mask.shape[1], c, c), lambda h, d, b: (d, 0, 0, 0)),
            pl.BlockSpec((c, c), lambda h, d, b: (0, 0)),
            pl.BlockSpec((1, GDN_DV), lambda h, d, b: (0, 0)),
        ],
        out_specs=pl.BlockSpec((tb, hw), lambda h, d, b: (d * jnp.maximum(b - 1, 0), h)),
        out_shape=jax.ShapeDtypeStruct((t, 2 * GDN_QK_HEADS * GDN_DV), BF16),
        scratch_shapes=[pltpu.VMEM((2, GDN_DK, GDN_DV), F32),
                        pltpu.VMEM((t, hw), F32),
                        pltpu.VMEM((2, nci, 2, GDN_DK + c, GDN_DK), BF16),
                        pltpu.VMEM((2, nci, 2, GDN_DK + c, GDN_DV), F32),
                        pltpu.VMEM((2, nci, V7X_SUBLANES, c), F32)],
        compiler_params=_cparams("arbitrary", "arbitrary", "arbitrary"),
        name="gdn_scan",
    )(qkv, qkv, qkv, proj, gt, alog, dtb,
      jnp.asarray(uv, BF16), jnp.asarray(mask, F32), jnp.asarray(eye, F32),
      onorm.reshape(1, GDN_DV))


def _gdn_gate_layout(ba, a_log, dt_bias, *, t):
    c = SCAN_CHUNK
    g = ba.reshape(t, 2, 2, GDN_QK_HEADS, 2)
    g = g.transpose(2, 3, 1, 4, 0).reshape(2, GDN_QK_HEADS, 4, t)
    g = jnp.concatenate([g, jnp.zeros_like(g)], axis=2)
    g = g.reshape(2, GDN_QK_HEADS, V7X_SUBLANES, t // c, c).transpose(0, 1, 3, 2, 4)
    g = jnp.stack([g[1], g[0]])

    def rows(pv):
        pv = pv.reshape(2, GDN_QK_HEADS, 2)
        z = jnp.zeros_like(pv)
        r = jnp.concatenate([z, pv, z, z], axis=2)
        r = jnp.broadcast_to(r[..., None], (2, GDN_QK_HEADS, V7X_SUBLANES, c))
        return jnp.stack([r[1], r[0]])

    return g, rows(a_log.astype(F32)), rows(dt_bias.astype(F32))


def _pad_cols(w, n):
    return jnp.concatenate([w, jnp.zeros((w.shape[0], n - w.shape[1]), w.dtype)], axis=1)


def _proj_tile(n):
    best = None
    for tn in range(PROJ_TN_MAX, 3 * V7X_MXU_COLS, -V7X_MXU_COLS):
        n_pad = -(-n // tn) * tn
        if best is None or n_pad < best[1]:
            best = (tn, n_pad)
    return best


def kernel(x, p, mixer_norm, gla_w_in, gla_w_gate_up, gla_b_gate, gla_out_norm, gla_w_out,
           gdn_w_in, gdn_conv, gdn_a_log, gdn_dt_bias, gdn_out_norm, gdn_w_out,
           ffn_norm, ffn_w_in, ffn_w_out, ple_norm, ple_w_gate, ple_w_proj, final_norm):
    _, t, dm = x.shape
    depth = mixer_norm.shape[0]
    h = x.reshape(t, dm)
    tm = min(t, 1024)
    tm_ple = min(t, 512)
    tn_out = 1024
    tf = 512
    tb_gdn = min(t, 1024)
    for i in range(depth):
        j = i // 2
        if i % 2 == 0:
            tn, n_pad = _proj_tile(gla_w_in.shape[2])
            w_in = _pad_cols(gla_w_in[j].astype(BF16), n_pad)
            proj = _norm_matmul(h, mixer_norm[i], w_in, tm=tm, tn=tn)
            wgu, bg = _gla_gate_weights(gla_w_gate_up[j], gla_b_gate[j])
            o = _gla_scan(proj, wgu, bg, gla_out_norm[j], t=t)
            h = _matmul_res(o, gla_w_out[j].astype(BF16), h, tm=tm, tn=tn_out)
        else:
            tn, n_pad = _proj_tile(gdn_w_in.shape[2])
            w_in = _pad_cols(gdn_w_in[j].astype(BF16), n_pad)
            proj = _norm_matmul(h, mixer_norm[i], w_in, tm=tm, tn=tn)
            kd = GDN_QK_HEADS * GDN_DK
            vd = 2 * GDN_QK_HEADS * GDN_DV
            qkv = _gdn_prep(proj, gdn_conv[j], t=t)
            ba = proj[:, 2 * kd + 2 * vd:2 * kd + 2 * vd + 4 * 2 * GDN_QK_HEADS]
            gt, alog, dtb = _gdn_gate_layout(ba, gdn_a_log[j], gdn_dt_bias[j], t=t)
            o = _gdn_scan(qkv, proj, gt, alog, dtb, gdn_out_norm[j], t=t, tb=tb_gdn)
            h = _matmul_res(o, gdn_w_out[j].astype(BF16), h, tm=tm, tn=tn_out)
        h = _ffn(h, ffn_norm[i], ffn_w_in[i].astype(BF16), ffn_w_out[i].astype(BF16), tm=tm, tf=tf)
        h = _ple(h, ple_norm[i], ple_w_gate[i].astype(BF16), p[i].reshape(t, -1),
                 ple_w_proj[i].astype(BF16), final_norm, tm=tm_ple, final=(i == depth - 1))
    return h.reshape(x.shape)
```

```python
import functools

import numpy as np
import jax
import jax.numpy as jnp
from jax import lax
from jax.experimental import pallas as pl
from jax.experimental.pallas import tpu as pltpu

F32 = jnp.float32
BF16 = jnp.bfloat16
EPS = 1e-6

V7X_LANES = 128
V7X_SUBLANES = 8
V7X_VMEM_LIMIT_BYTES = 56 * 1024 * 1024

GLA_HEADS = 4
GLA_DK = 256
GLA_DV = 512
GLA_RANK = 16
GLA_GATE_NORM = 16.0
GDN_QK_HEADS = 16
GDN_DK = 128
GDN_DV = 128
CONV_W = 5
SCAN_CHUNK = 128
LOG2_E = 1.4426950408889634
UNIT_CHUNKS = 8
PREP_ROWS = 64
V7X_MXU_COLS = 256
PROJ_TN_MAX = 1792


def _cparams(*sem):
    return pltpu.CompilerParams(dimension_semantics=sem,
                                vmem_limit_bytes=V7X_VMEM_LIMIT_BYTES)


def _dot(a, b):
    return jnp.dot(a, b, preferred_element_type=F32)


def _dot_nt(a, b):
    return lax.dot_general(a, b, (((1,), (1,)), ((), ())), preferred_element_type=F32)


def _dot_tn(a, b):
    return lax.dot_general(a, b, (((0,), (0,)), ((), ())), preferred_element_type=F32)


def _rms_scale(x):
    return lax.rsqrt(jnp.mean(x * x, axis=-1, keepdims=True) + EPS)


def _sigmoid(x):
    return 1.0 / (1.0 + jnp.exp(-x))


def _softplus(x):
    return jnp.maximum(x, 0.0) + jnp.log(1.0 + jnp.exp(-jnp.abs(x)))


def _split3(x):
    p1 = x.astype(BF16)
    r = x - p1.astype(F32)
    p2 = r.astype(BF16)
    p3 = (r - p2.astype(F32)).astype(BF16)
    return p1, p2, p3


def _norm_matmul_kernel(x_ref, nw_ref, w_ref, o_ref, xn_ref):
    @pl.when(pl.program_id(1) == 0)
    def _():
        x = x_ref[...]
        xn_ref[...] = (x * _rms_scale(x) * nw_ref[...]).astype(BF16)

    o_ref[...] = _dot(xn_ref[...], w_ref[...]).astype(o_ref.dtype)


def _norm_matmul(x, nw, w, *, tm, tn):
    t, d = x.shape
    n = w.shape[1]
    return pl.pallas_call(
        _norm_matmul_kernel,
        grid=(t // tm, n // tn),
        in_specs=[pl.BlockSpec((tm, d), lambda i, j: (i, 0)),
                  pl.BlockSpec((1, d), lambda i, j: (0, 0)),
                  pl.BlockSpec((d, tn), lambda i, j: (0, j))],
        out_specs=pl.BlockSpec((tm, tn), lambda i, j: (i, j)),
        out_shape=jax.ShapeDtypeStruct((t, n), F32),
        scratch_shapes=[pltpu.VMEM((tm, d), BF16)],
        compiler_params=_cparams("parallel", "arbitrary"),
        name="norm_matmul",
    )(x, nw.reshape(1, d), w)


def _matmul_res_kernel(a_ref, w_ref, h_ref, o_ref):
    o_ref[...] = h_ref[...] + _dot(a_ref[...], w_ref[...])


def _matmul_res(a, w, h, *, tm, tn):
    t, k = a.shape
    n = w.shape[1]
    return pl.pallas_call(
        _matmul_res_kernel,
        grid=(t // tm, n // tn),
        in_specs=[pl.BlockSpec((tm, k), lambda i, j: (i, 0)),
                  pl.BlockSpec((k, tn), lambda i, j: (0, j)),
                  pl.BlockSpec((tm, tn), lambda i, j: (i, j))],
        out_specs=pl.BlockSpec((tm, tn), lambda i, j: (i, j)),
        out_shape=jax.ShapeDtypeStruct((t, n), F32),
        compiler_params=_cparams("parallel", "arbitrary"),
        name="matmul_res",
    )(a, w, h)


def _ffn_kernel(x_ref, nw_ref, wg_ref, wu_ref, wo_ref, o_ref, xn_ref):
    @pl.when(pl.program_id(1) == 0)
    def _():
        x = x_ref[...]
        xn_ref[...] = (x * _rms_scale(x) * nw_ref[...]).astype(BF16)
        o_ref[...] = x

    xn = xn_ref[...]
    g = _dot(xn, wg_ref[...])
    u = _dot(xn, wu_ref[...])
    act = (g * _sigmoid(g) * u).astype(BF16)
    o_ref[...] += _dot(act, wo_ref[...])


def _ffn(x, nw, w_in, w_out, *, tm, tf):
    t, d = x.shape
    f = w_out.shape[0]
    nf = f // tf
    return pl.pallas_call(
        _ffn_kernel,
        grid=(t // tm, nf),
        in_specs=[pl.BlockSpec((tm, d), lambda i, j: (i, 0), pipeline_mode=pl.Buffered(1)),
                  pl.BlockSpec((1, d), lambda i, j: (0, 0)),
                  pl.BlockSpec((d, tf), lambda i, j: (0, j)),
                  pl.BlockSpec((d, tf), lambda i, j: (0, nf + j)),
                  pl.BlockSpec((tf, d), lambda i, j: (j, 0))],
        out_specs=pl.BlockSpec((tm, d), lambda i, j: (i, 0)),
        out_shape=jax.ShapeDtypeStruct((t, d), F32),
        scratch_shapes=[pltpu.VMEM((tm, d), BF16)],
        compiler_params=_cparams("parallel", "arbitrary"),
        name="ffn",
    )(x, nw.reshape(1, d), w_in, w_in, w_out)


def _ple_kernel(h_ref, nw_ref, wg_ref, p_ref, wp_ref, fn_ref, o_ref, *, final):
    h = h_ref[...]
    xn = (h * _rms_scale(h) * nw_ref[...]).astype(BF16)
    gate = _sigmoid(_dot(xn, wg_ref[...]))
    y = h + gate * _dot(p_ref[...].astype(BF16), wp_ref[...])
    if final:
        y = y * _rms_scale(y) * fn_ref[...]
    o_ref[...] = y


def _ple(h, nw, wg, p, wp, fn, *, tm, final):
    t, d = h.shape
    kp = p.shape[1]
    return pl.pallas_call(
        functools.partial(_ple_kernel, final=final),
        grid=(t // tm,),
        in_specs=[pl.BlockSpec((tm, d), lambda i: (i, 0)),
                  pl.BlockSpec((1, d), lambda i: (0, 0)),
                  pl.BlockSpec((d, d), lambda i: (0, 0), pipeline_mode=pl.Buffered(1)),
                  pl.BlockSpec((tm, kp), lambda i: (i, 0)),
                  pl.BlockSpec((kp, d), lambda i: (0, 0), pipeline_mode=pl.Buffered(1)),
                  pl.BlockSpec((1, d), lambda i: (0, 0))],
        out_specs=pl.BlockSpec((tm, d), lambda i: (i, 0)),
        out_shape=jax.ShapeDtypeStruct((t, d), F32),
        compiler_params=_cparams("parallel"),
        name="ple",
    )(h, nw.reshape(1, d), wg, p, wp, fn.reshape(1, d))


def _gla_levels(c):
    out, b = [], c // 2
    while b >= 1:
        out.append(b)
        b //= 2
    return out


def _gla_consts(c):
    idx = np.arange(c)
    i, m = idx[:, None], idx[None, :]
    blocks = [(m <= i)]
    masks = []
    for b in _gla_levels(c):
        mid = (idx // (2 * b)) * (2 * b) + b - 1
        md = mid[:, None]
        blocks.append(np.where(i > md, (m > md) & (m <= i), (m > i) & (m <= md)))
        masks.append((i // (2 * b) == m // (2 * b)) & (i % (2 * b) >= b) & (m % (2 * b) < b))
    masks.append(i == m)
    fwd_w = np.concatenate([bl.astype(np.float32) for bl in blocks], axis=0)
    bwd_w = np.concatenate([bl.astype(np.float32)[::-1, ::-1] for bl in blocks], axis=0)
    ones = np.ones((V7X_SUBLANES, c), np.float32)
    wcum = np.stack([np.concatenate([bwd_w, ones], 0), np.concatenate([fwd_w, ones], 0)])
    fwd_m = np.stack([mk.astype(np.float32) for mk in masks])
    bwd_m = np.stack([mk.astype(np.float32)[::-1, ::-1] for mk in masks])
    mask = np.stack([bwd_m, fwd_m])
    return wcum, mask


def _gla_kernel(q_ref, k_ref, v_ref, og_ref, lr_ref, wgu_ref, bg_ref, wcum_ref, mask_ref,
                onorm_ref, o_ref, st_ref, obwd_ref, qd_ref, oa_ref, dl_ref, et_ref,
                *, c, nci, tb):
    d = pl.program_id(1)
    b = pl.program_id(2)
    nb = pl.num_programs(2)
    nlev = len(_gla_levels(c))

    @pl.when(b == 0)
    def _():
        st_ref[...] = jnp.zeros_like(st_ref)

    def prepare():
        chunks = range(nci)
        rows = [slice(ci * c, (ci + 1) * c) for ci in chunks]
        q = [q_ref[r, :] * (GLA_DK ** -0.5) for r in rows]
        k = [k_ref[r, :] for r in rows]
        vb = [v_ref[r, :].astype(BF16) for r in rows]
        gk = [_dot(lr_ref[r, :].astype(BF16), wgu_ref[...]) + bg_ref[...] for r in rows]
        g = [(jnp.minimum(x, 0.0) - jnp.log(1.0 + jnp.exp(-jnp.abs(x)))) * (LOG2_E / GLA_GATE_NORM) for x in gk]
        gp = [jnp.concatenate(_split3(x)[:2], axis=1) for x in g]
        cums = [_dot(wcum_ref[...], x) for x in gp]
        cums = [x[:, :GLA_DK] + x[:, GLA_DK:] for x in cums]
        cum = [x[0:c] for x in cums]
        tot = [x[(1 + nlev) * c:(1 + nlev) * c + V7X_SUBLANES] for x in cums]
        rest = [tt[0:1] - cc for tt, cc in zip(tot, cum)]
        for ci in chunks:
            et_ref[ci] = jnp.exp2(tot[ci])

        a = [jnp.sum(qq * kk, axis=-1, keepdims=True) * mask_ref[nlev] for qq, kk in zip(q, k)]
        for lv in range(nlev):
            e = [jnp.exp2(x[(1 + lv) * c:(2 + lv) * c]) for x in cums]
            s = [_dot_nt((qq * ee).astype(BF16), (kk * ee).astype(BF16)) for qq, kk, ee in zip(q, k, e)]
            a = [aa + ss * mask_ref[lv] for aa, ss in zip(a, s)]
        a = [aa.astype(BF16) for aa in a]

        kd = [(kk * jnp.exp2(rr)).astype(BF16) for kk, rr in zip(k, rest)]
        for ci in chunks:
            qd_ref[ci] = (q[ci] * jnp.exp2(cum[ci])).astype(BF16)
            oa_ref[ci] = _dot(a[ci], vb[ci])
            dl_ref[ci] = _dot_tn(vb[ci], kd[ci])

    def walk(order, blk, forward):
        for ci in order:
            rows = slice(ci * c, (ci + 1) * c)
            st = st_ref[...]
            o = _dot_nt(qd_ref[ci], st.astype(BF16)) + oa_ref[ci]
            st_ref[...] = st * et_ref[ci][0:1] + dl_ref[ci]
            orow = pl.ds(pl.multiple_of(blk * tb + ci * c, c), c)
            if forward:
                ot = o + obwd_ref[orow, :]
                y = ot * _rms_scale(ot) * onorm_ref[...]
                og = og_ref[rows, :]
                o_ref[rows, :] = (y * (og * _sigmoid(og))).astype(o_ref.dtype)
            else:
                obwd_ref[orow, :] = o

    prepare()

    @pl.when(d == 0)
    def _():
        walk(range(nci - 1, -1, -1), nb - 1 - b, False)

    @pl.when(d == 1)
    def _():
        walk(range(nci), b, True)


def _gla_scan(proj, wgu, bg, onorm, *, t):
    c = SCAN_CHUNK
    tb = min(t, 512)
    nb = t // tb
    nci = tb // c
    wcum, mask = _gla_consts(c)
    nlev = len(_gla_levels(c))
    qoff = 0
    koff = GLA_HEADS * GLA_DK // GLA_DK
    voff = 2 * GLA_HEADS * GLA_DK // GLA_DV
    goff = voff + GLA_HEADS
    lroff = (2 * GLA_HEADS * GLA_DK + 2 * GLA_HEADS * GLA_DV) // V7X_LANES

    def tblk(d, b):
        return d * b + (1 - d) * (nb - 1 - b)

    return pl.pallas_call(
        functools.partial(_gla_kernel, c=c, nci=nci, tb=tb),
        grid=(GLA_HEADS, 2, nb),
        in_specs=[
            pl.BlockSpec((tb, GLA_DK), lambda h, d, b: (tblk(d, b), qoff + h)),
            pl.BlockSpec((tb, GLA_DK), lambda h, d, b: (tblk(d, b), koff + h)),
            pl.BlockSpec((tb, GLA_DV), lambda h, d, b: (tblk(d, b), voff + h)),
            pl.BlockSpec((tb, GLA_DV), lambda h, d, b: (tblk(d, b), goff + h)),
            pl.BlockSpec((tb, V7X_LANES), lambda h, d, b: (tblk(d, b), lroff)),
            pl.BlockSpec((None, None, V7X_LANES, GLA_DK), lambda h, d, b: (d, h, 0, 0)),
            pl.BlockSpec((None, None, 1, GLA_DK), lambda h, d, b: (d, h, 0, 0)),
            pl.BlockSpec((None, wcum.shape[1], c), lambda h, d, b: (d, 0, 0)),
            pl.BlockSpec((None, nlev + 1, c, c), lambda h, d, b: (d, 0, 0, 0)),
            pl.BlockSpec((1, GLA_DV), lambda h, d, b: (0, 0)),
        ],
        out_specs=pl.BlockSpec((tb, GLA_DV), lambda h, d, b: (d * b, h)),
        out_shape=jax.ShapeDtypeStruct((t, GLA_HEADS * GLA_DV), BF16),
        scratch_shapes=[pltpu.VMEM((GLA_DV, GLA_DK), F32),
                        pltpu.VMEM((t, GLA_DV), F32),
                        pltpu.VMEM((nci, c, GLA_DK), BF16),
                        pltpu.VMEM((nci, c, GLA_DV), F32),
                        pltpu.VMEM((nci, GLA_DV, GLA_DK), F32),
                        pltpu.VMEM((nci, V7X_SUBLANES, GLA_DK), F32)],
        compiler_params=_cparams("arbitrary", "arbitrary", "arbitrary"),
        name="gla_scan",
    )(proj, proj, proj, proj, proj, wgu, bg,
      jnp.asarray(wcum, BF16), jnp.asarray(mask, F32),
      onorm.reshape(1, GLA_DV))


def _gla_gate_weights(w_gate_up, b_gate):
    w = w_gate_up.reshape(2, GLA_RANK, GLA_HEADS, GLA_DK).transpose(0, 2, 1, 3)
    z = jnp.zeros((GLA_HEADS, V7X_LANES, GLA_DK), F32)
    fwd = z.at[:, 0:GLA_RANK].set(w[0])
    bwd = z.at[:, GLA_RANK:2 * GLA_RANK].set(w[1])
    wgu = jnp.stack([bwd, fwd]).astype(BF16)
    bg = jnp.stack([b_gate[1], b_gate[0]]).reshape(2, GLA_HEADS, 1, GLA_DK)
    return wgu, bg


def _gdn_qkv_kernel(prev_ref, cur_ref, next_ref, nw_ref, w_ref, cw_ref, o_ref, xn_ref, *r_refs, tm, nq, nqk):
    i = pl.program_id(0)
    j = pl.program_id(1)
    halo = V7X_SUBLANES

    @pl.when(j == 0)
    def _():
        def normed(x):
            return (x * _rms_scale(x) * nw_ref[...]).astype(BF16)

        zero = jnp.zeros((halo, xn_ref.shape[1]), BF16)
        xn_ref[0:halo, :] = jnp.where(i > 0, normed(prev_ref[...]), zero)
        xn_ref[halo:halo + tm, :] = normed(cur_ref[...])
        xn_ref[halo + tm:2 * halo + tm, :] = jnp.where(i < pl.num_programs(0) - 1, normed(next_ref[...]), zero)

    is_qk = j < nqk
    qs = jnp.where(j < nq, GDN_DK ** -0.5, 1.0)
    n = PREP_ROWS + 2 * halo
    pw = V7X_MXU_COLS
    npieces = w_ref.shape[1] // pw

    nslab = tm // (4 * PREP_ROWS)
    edges = [0] + [(s + 1) * 4 * PREP_ROWS + 2 * halo for s in range(nslab)]

    def project(p, s):
        rows = slice(edges[s], edges[s + 1])
        r_refs[p][rows, :] = _dot(xn_ref[rows, :], w_ref[:, p * pw:(p + 1) * pw])

    def conv(p, s):
        for r0 in range(s * 4 * PREP_ROWS, (s + 1) * 4 * PREP_ROWS, PREP_ROWS):
            for hh in range(pw // GDN_DK):
                cols = slice(p * pw + hh * GDN_DK, p * pw + (hh + 1) * GDN_DK)
                xs = r_refs[p][r0:r0 + n, hh * GDN_DK:(hh + 1) * GDN_DK]
                acc = jnp.zeros((PREP_ROWS, GDN_DK), F32)
                for w in range(CONV_W):
                    sh = xs if w == CONV_W // 2 else pltpu.roll(xs, (CONV_W // 2 - w) % n, axis=0)
                    acc = acc + sh[halo:halo + PREP_ROWS] * cw_ref[w:w + 1, cols]
                y = acc * _sigmoid(acc)
                ss = jnp.sum(y * y, axis=-1, keepdims=True)
                scale = jnp.where(is_qk, lax.rsqrt(ss + EPS) * qs, 1.0)
                o_ref[r0:r0 + PREP_ROWS, cols] = (y * scale).astype(o_ref.dtype)

    for s in range(nslab):
        project(0, s)
    for p in range(1, npieces):
        for s in range(nslab):
            project(p, s)
            conv(p - 1, s)
    for s in range(nslab):
        conv(npieces - 1, s)


def _gdn_qkv(x, nw, w, conv_w, *, tm, tn):
    t, d = x.shape
    ncols = conv_w.shape[1]
    hb = tm // V7X_SUBLANES
    nt = t // tm
    nq = GDN_QK_HEADS * GDN_DK // tn
    return pl.pallas_call(
        functools.partial(_gdn_qkv_kernel, tm=tm, nq=nq, nqk=2 * nq),
        grid=(nt, ncols // tn),
        in_specs=[
            pl.BlockSpec((V7X_SUBLANES, d), lambda i, j: (jnp.maximum(i * hb - 1, 0), 0)),
            pl.BlockSpec((tm, d), lambda i, j: (i, 0)),
            pl.BlockSpec((V7X_SUBLANES, d), lambda i, j: (jnp.minimum((i + 1) * hb, nt * hb - 1), 0)),
            pl.BlockSpec((1, d), lambda i, j: (0, 0)),
            pl.BlockSpec((d, tn), lambda i, j: (0, j)),
            pl.BlockSpec((CONV_W, tn), lambda i, j: (0, j)),
        ],
        out_specs=pl.BlockSpec((tm, tn), lambda i, j: (i, j)),
        out_shape=jax.ShapeDtypeStruct((t, ncols), BF16),
        scratch_shapes=[pltpu.VMEM((tm + 2 * V7X_SUBLANES, d), BF16)]
        + [pltpu.VMEM((tm + 2 * V7X_SUBLANES, V7X_MXU_COLS), F32) for _ in range(tn // V7X_MXU_COLS)],
        compiler_params=_cparams("parallel", "arbitrary"),
        name="gdn_qkv",
    )(x, x, x, nw.reshape(1, d), w, conv_w)


def _gdn_consts(c):
    idx = np.arange(c)
    i, m = idx[:, None], idx[None, :]
    u_f = (i <= m)
    uv_f = np.concatenate([u_f, ~u_f], axis=1).astype(np.float32)
    u_b = (i >= m)
    uv_b = np.concatenate([u_b, ~u_b], axis=1).astype(np.float32)
    uv = np.stack([uv_b, uv_f])

    def pair(b):
        return (i // (2 * b) == m // (2 * b)) & (i // b != m // b)

    lower, strict = (m <= i), (m < i)
    blk = (i // 16 == m // 16)
    levels = []
    b = 16
    while 2 * b <= c:
        levels.append(pair(b))
        b *= 2
    fwd = [lower, strict, blk & strict] + [p & strict for p in levels]
    mask_f = np.stack([x.astype(np.float32) for x in fwd])
    mask_b = np.stack([x.astype(np.float32)[::-1, ::-1] for x in fwd])
    mask = np.stack([mask_b, mask_f])
    return uv, mask, np.eye(c, dtype=np.float32), len(levels)


def _gdn_kernel(q_ref, k_ref, v_ref, z_ref, gt_ref, alog_ref, dtb_ref, uv_ref, mask_ref,
                eye_ref, onorm_ref, o_ref, s_ref, obwd_ref, mq_ref, bo_ref, et_ref,
                *, c, nci, tb, nb, nlev):
    hq = pl.program_id(0)
    d = pl.program_id(1)
    b = pl.program_id(2)

    @pl.when((hq == 0) & (d == 0) & (b == 0))
    def _():
        obwd_ref[...] = jnp.zeros_like(obwd_ref)
        s_ref[...] = jnp.zeros_like(s_ref)
        mq_ref[...] = jnp.zeros_like(mq_ref)
        bo_ref[...] = jnp.zeros_like(bo_ref)
        et_ref[...] = jnp.zeros_like(et_ref)

    row8 = lax.broadcasted_iota(jnp.int32, (V7X_SUBLANES, c), 0)

    def prepare(slot, chunk_ids):
        chunks = range(len(chunk_ids))
        units = [(ci, hh) for ci in chunks for hh in range(2)]
        rows = [slice(ca * c, (ca + 1) * c) for ca in chunk_ids]
        gts = [gt_ref[ca] for ca in chunk_ids]
        beta8 = [_sigmoid(gt) for gt in gts]
        g8 = [-jnp.exp(alog_ref[...]) * _softplus(gt + dtb_ref[...]) for gt in gts]
        gp = [jnp.concatenate(_split3(g), axis=0) for g in g8]
        qb = [q_ref[r, :] for r in rows]
        kb = [k_ref[r, :] for r in rows]
        qf = [q.astype(F32) for q in qb]
        kf = [k.astype(F32) for k in kb]
        cl = [_dot(g, uv_ref[...]) for g in gp]
        kk = [_dot_nt(k, k) for k in kb]
        qk = [_dot_nt(q, k) for q, k in zip(qb, kb)]
        kt = [k.T for k in kf]
        cl = [x[0:8] + x[8:16] + x[16:24] for x in cl]
        cum8 = [x[:, :c] for x in cl]
        rest8 = [x[:, c:] for x in cl]
        cols = [jnp.concatenate([jnp.where(row8 < 2, bt, cm), jnp.zeros((c - V7X_SUBLANES, c), F32)],
                                axis=0).T for bt, cm in zip(beta8, cum8)]
        for ci in chunks:
            et_ref[slot, chunk_ids[ci]] = jnp.exp(cum8[ci] + rest8[ci])
        yield

        beta_c = [cols[ci][:, hh:hh + 1] for ci, hh in units]
        cum_c = [cols[ci][:, 2 + hh:3 + hh] for ci, hh in units]
        dec = [jnp.exp(jnp.minimum(cc - cum8[ci][2 + hh:3 + hh, :], 0.0))
               for cc, (ci, hh) in zip(cum_c, units)]
        lm = [kk[ci] * dc * bc * mask_ref[1] for dc, bc, (ci, hh) in zip(dec, beta_c, units)]
        at = [(qk[ci] * dc * mask_ref[0]).astype(BF16) for dc, (ci, hh) in zip(dec, units)]

        n1 = [-(l * mask_ref[2]) for l in lm]
        x = [eye_ref[...] + n for n in n1]
        p = [n.astype(BF16) for n in n1]
        yield
        for _ in range(3):
            p = [_dot(pp, pp).astype(BF16) for pp in p]
            x = [xx + _dot(xx.astype(BF16), pp) for xx, pp in zip(x, p)]
            yield
        for lv in range(nlev):
            xb = [xx.astype(BF16) for xx in x]
            t1 = [_dot((l * mask_ref[3 + lv]).astype(BF16), bb).astype(BF16) for l, bb in zip(lm, xb)]
            x = [xx - _dot(bb, tt) for xx, bb, tt in zip(x, xb, t1)]
            yield

        e_c = [jnp.exp(cc) for cc in cum_c]
        rhs = [jnp.concatenate([v_ref[rows[ci], hh * GDN_DV:(hh + 1) * GDN_DV].astype(F32) * bc,
                                kf[ci] * (bc * ec)], axis=1).astype(BF16)
               for bc, ec, (ci, hh) in zip(beta_c, e_c, units)]
        uw = [_dot(xx.astype(BF16), r).astype(BF16) for xx, r in zip(x, rhs)]
        yield
        kdt = [(kt[ci] * jnp.exp(rest8[ci][2 + hh:3 + hh, :])).astype(BF16) for ci, hh in units]
        pr = [_dot(jnp.concatenate([kd, a], axis=0), w) for kd, a, w in zip(kdt, at, uw)]
        for n, (ci, hh) in enumerate(units):
            m = -pr[n][:GDN_DK, GDN_DV:]
            qm = qf[ci] * e_c[n] - pr[n][GDN_DK:, GDN_DV:]
            mq_ref[slot, chunk_ids[ci], hh] = jnp.concatenate([m, qm], axis=0).astype(BF16)
            bo_ref[slot, chunk_ids[ci], hh] = pr[n][:, :GDN_DV]

    forward = d == 1

    def walk_step(i, slot, wb):
        ci = jnp.where(forward, i, nci - 1 - i)
        r0 = pl.multiple_of(ci * c, c)
        rows = pl.ds(r0, c)
        orow = pl.ds(pl.multiple_of(wb * tb + r0, c), c)
        et = et_ref[slot, ci]
        for hh in range(2):
            s = s_ref[hh]
            r = _dot(mq_ref[slot, ci, hh], s.astype(BF16)) + bo_ref[slot, ci, hh]
            s_ref[hh] = s * et[2 + hh:3 + hh, :] + r[:GDN_DK]
            o = r[GDN_DK:]
            ocol = slice(hh * GDN_DV, (hh + 1) * GDN_DV)
            parked = obwd_ref[orow, ocol]
            ot = o + jnp.where(forward, parked, 0.0)
            obwd_ref[orow, ocol] = jnp.where(forward, parked, o)
            y = ot * _rms_scale(ot) * onorm_ref[...]
            z = z_ref[rows, ocol]
            o_ref[rows, ocol] = (y * (z * _sigmoid(z))).astype(o_ref.dtype)

    def time_block(step_b):
        return jnp.where(forward, step_b, nb - 1 - step_b)

    @pl.when(b < nb)
    def _():
        wb = time_block(jnp.maximum(b - 1, 0))
        ngrp = -(-nci // UNIT_CHUNKS)
        per = nci // ngrp
        for g in range(ngrp):
            stages = prepare(b % 2, list(range(g * per, (g + 1) * per)))
            for i in range(g * per, (g + 1) * per):
                next(stages)
                walk_step(i, (b + 1) % 2, wb)
            for _ in stages:
                pass

    @pl.when(b == 0)
    def _():
        s_ref[...] = jnp.zeros_like(s_ref)

    @pl.when(b == nb)
    def _():
        for i in range(nci):
            walk_step(i, (nb - 1) % 2, time_block(nb - 1))


def _gdn_scan(qkv, proj, gt, alog, dtb, onorm, *, t, tb):
    c = SCAN_CHUNK
    nb = t // tb
    nci = tb // c
    uv, mask, eye, nlev = _gdn_consts(c)
    hw = 2 * GDN_DV
    voff = 2 * GDN_QK_HEADS * GDN_DK // hw

    def tblk(d, b):
        return d * b + (1 - d) * (nb - 1 - b)

    def prep_blk(d, b):
        return tblk(d, jnp.minimum(b, nb - 1))

    def walk_blk(d, b):
        return tblk(d, jnp.maximum(b - 1, 0))

    return pl.pallas_call(
        functools.partial(_gdn_kernel, c=c, nci=nci, tb=tb, nb=nb, nlev=nlev),
        grid=(GDN_QK_HEADS, 2, nb + 1),
        in_specs=[
            pl.BlockSpec((tb, GDN_DK), lambda h, d, b: (prep_blk(d, b), h)),
            pl.BlockSpec((tb, GDN_DK), lambda h, d, b: (prep_blk(d, b), GDN_QK_HEADS + h)),
            pl.BlockSpec((tb, hw), lambda h, d, b: (prep_blk(d, b), voff + h)),
            pl.BlockSpec((tb, hw), lambda h, d, b: (walk_blk(d, b), h)),
            pl.BlockSpec((None, None, nci, V7X_SUBLANES, c), lambda h, d, b: (d, h, prep_blk(d, b), 0, 0)),
            pl.BlockSpec((None, None, V7X_SUBLANES, c), lambda h, d, b: (d, h, 0, 0)),
            pl.BlockSpec((None, None, V7X_SUBLANES, c), lambda h, d, b: (d, h, 0, 0)),
            pl.BlockSpec((None, c, 2 * c), lambda h, d, b: (d, 0, 0)),
            pl.BlockSpec((None, mask.shape[1], c, c), lambda h, d, b: (d, 0, 0, 0)),
            pl.BlockSpec((c, c), lambda h, d, b: (0, 0)),
            pl.BlockSpec((1, GDN_DV), lambda h, d, b: (0, 0)),
        ],
        out_specs=pl.BlockSpec((tb, hw), lambda h, d, b: (d * jnp.maximum(b - 1, 0), h)),
        out_shape=jax.ShapeDtypeStruct((t, 2 * GDN_QK_HEADS * GDN_DV), BF16),
        scratch_shapes=[pltpu.VMEM((2, GDN_DK, GDN_DV), F32),
                        pltpu.VMEM((t, hw), F32),
                        pltpu.VMEM((2, nci, 2, GDN_DK + c, GDN_DK), BF16),
                        pltpu.VMEM((2, nci, 2, GDN_DK + c, GDN_DV), F32),
                        pltpu.VMEM((2, nci, V7X_SUBLANES, c), F32)],
        compiler_params=_cparams("arbitrary", "arbitrary", "arbitrary"),
        name="gdn_scan",
    )(qkv, qkv, qkv, proj, gt, alog, dtb,
      jnp.asarray(uv, BF16), jnp.asarray(mask, F32), jnp.asarray(eye, F32),
      onorm.reshape(1, GDN_DV))


def _gdn_gate_layout(ba, a_log, dt_bias, *, t):
    c = SCAN_CHUNK
    g = ba.reshape(t, 2, 2, GDN_QK_HEADS, 2)
    g = g.transpose(2, 3, 1, 4, 0).reshape(2, GDN_QK_HEADS, 4, t)
    g = jnp.concatenate([g, jnp.zeros_like(g)], axis=2)
    g = g.reshape(2, GDN_QK_HEADS, V7X_SUBLANES, t // c, c).transpose(0, 1, 3, 2, 4)
    g = jnp.stack([g[1], g[0]])

    def rows(pv):
        pv = pv.reshape(2, GDN_QK_HEADS, 2)
        z = jnp.zeros_like(pv)
        r = jnp.concatenate([z, pv, z, z], axis=2)
        r = jnp.broadcast_to(r[..., None], (2, GDN_QK_HEADS, V7X_SUBLANES, c))
        return jnp.stack([r[1], r[0]])

    return g, rows(a_log.astype(F32)), rows(dt_bias.astype(F32))


def _pad_cols(w, n):
    return jnp.concatenate([w, jnp.zeros((w.shape[0], n - w.shape[1]), w.dtype)], axis=1)


def _proj_tile(n):
    best = None
    for tn in range(PROJ_TN_MAX, 3 * V7X_MXU_COLS, -V7X_MXU_COLS):
        n_pad = -(-n // tn) * tn
        if best is None or n_pad < best[1]:
            best = (tn, n_pad)
    return best


def kernel(x, p, mixer_norm, gla_w_in, gla_w_gate_up, gla_b_gate, gla_out_norm, gla_w_out,
           gdn_w_in, gdn_conv, gdn_a_log, gdn_dt_bias, gdn_out_norm, gdn_w_out,
           ffn_norm, ffn_w_in, ffn_w_out, ple_norm, ple_w_gate, ple_w_proj, final_norm):
    _, t, dm = x.shape
    depth = mixer_norm.shape[0]
    h = x.reshape(t, dm)
    tm = min(t, 1024)
    tm_ple = min(t, 512)
    tn_out = 1024
    tf = 512
    tn_qkv = 1024
    tb_gdn = min(t, 1024)
    for i in range(depth):
        j = i // 2
        if i % 2 == 0:
            tn, n_pad = _proj_tile(gla_w_in.shape[2])
            w_in = _pad_cols(gla_w_in[j].astype(BF16), n_pad)
            proj = _norm_matmul(h, mixer_norm[i], w_in, tm=tm, tn=tn)
            wgu, bg = _gla_gate_weights(gla_w_gate_up[j], gla_b_gate[j])
            o = _gla_scan(proj, wgu, bg, gla_out_norm[j], t=t)
            h = _matmul_res(o, gla_w_out[j].astype(BF16), h, tm=tm, tn=tn_out)
        else:
            nconv = gdn_conv.shape[2]
            vd = 2 * GDN_QK_HEADS * GDN_DV
            w_bf = gdn_w_in[j].astype(BF16)
            qkv = _gdn_qkv(h, mixer_norm[i], w_bf[:, :nconv], gdn_conv[j], tm=tm, tn=tn_qkv)
            tn, n_pad = _proj_tile(gdn_w_in.shape[2] - nconv)
            zba = _norm_matmul(h, mixer_norm[i], _pad_cols(w_bf[:, nconv:], n_pad), tm=tm, tn=tn)
            gt, alog, dtb = _gdn_gate_layout(zba[:, vd:vd + 4 * 2 * GDN_QK_HEADS], gdn_a_log[j], gdn_dt_bias[j], t=t)
            o = _gdn_scan(qkv, zba, gt, alog, dtb, gdn_out_norm[j], t=t, tb=tb_gdn)
            h = _matmul_res(o, gdn_w_out[j].astype(BF16), h, tm=tm, tn=tn_out)
        h = _ffn(h, ffn_norm[i], ffn_w_in[i].astype(BF16), ffn_w_out[i].astype(BF16), tm=tm, tf=tf)
        h = _ple(h, ple_norm[i], ple_w_gate[i].astype(BF16), p[i].reshape(t, -1),
                 ple_w_proj[i].astype(BF16), final_norm, tm=tm_ple, final=(i == depth - 1))
    return h.reshape(x.shape)
```

```python
import functools

import numpy as np
import jax
import jax.numpy as jnp
from jax import lax
from jax.experimental import pallas as pl
from jax.experimental.pallas import tpu as pltpu

F32 = jnp.float32
BF16 = jnp.bfloat16
EPS = 1e-6

V7X_LANES = 128
V7X_SUBLANES = 8
V7X_VMEM_LIMIT_BYTES = 56 * 1024 * 1024

GLA_HEADS = 4
GLA_DK = 256
GLA_DV = 512
GLA_RANK = 16
GLA_GATE_NORM = 16.0
GDN_QK_HEADS = 16
GDN_DK = 128
GDN_DV = 128
CONV_W = 5
SCAN_CHUNK = 128
LOG2_E = 1.4426950408889634
UNIT_CHUNKS = 8
SLAB_ROWS = 256
PREP_ROWS = 256
V7X_MXU_COLS = 256
PROJ_TN_MAX = 1792


def _cparams(*sem):
    return pltpu.CompilerParams(dimension_semantics=sem,
                                vmem_limit_bytes=V7X_VMEM_LIMIT_BYTES)


def _dot(a, b):
    return jnp.dot(a, b, preferred_element_type=F32)


def _dot_nt(a, b):
    return lax.dot_general(a, b, (((1,), (1,)), ((), ())), preferred_element_type=F32)


def _dot_tn(a, b):
    return lax.dot_general(a, b, (((0,), (0,)), ((), ())), preferred_element_type=F32)


def _rms_scale(x):
    return lax.rsqrt(jnp.mean(x * x, axis=-1, keepdims=True) + EPS)


def _sigmoid(x):
    return 1.0 / (1.0 + jnp.exp(-x))


def _softplus(x):
    return jnp.maximum(x, 0.0) + jnp.log(1.0 + jnp.exp(-jnp.abs(x)))


def _split3(x):
    p1 = x.astype(BF16)
    r = x - p1.astype(F32)
    p2 = r.astype(BF16)
    p3 = (r - p2.astype(F32)).astype(BF16)
    return p1, p2, p3


def _norm_matmul_kernel(x_ref, nw_ref, w_ref, o_ref, xn_ref):
    @pl.when(pl.program_id(1) == 0)
    def _():
        x = x_ref[...]
        xn_ref[...] = (x * _rms_scale(x) * nw_ref[...]).astype(BF16)

    o_ref[...] = _dot(xn_ref[...], w_ref[...]).astype(o_ref.dtype)


def _norm_matmul(x, nw, w, *, tm, tn):
    t, d = x.shape
    n = w.shape[1]
    return pl.pallas_call(
        _norm_matmul_kernel,
        grid=(t // tm, n // tn),
        in_specs=[pl.BlockSpec((tm, d), lambda i, j: (i, 0)),
                  pl.BlockSpec((1, d), lambda i, j: (0, 0)),
                  pl.BlockSpec((d, tn), lambda i, j: (0, j))],
        out_specs=pl.BlockSpec((tm, tn), lambda i, j: (i, j)),
        out_shape=jax.ShapeDtypeStruct((t, n), F32),
        scratch_shapes=[pltpu.VMEM((tm, d), BF16)],
        compiler_params=_cparams("parallel", "arbitrary"),
        name="norm_matmul",
    )(x, nw.reshape(1, d), w)


def _matmul_res_kernel(a_ref, w_ref, h_ref, o_ref):
    o_ref[...] = h_ref[...] + _dot(a_ref[...], w_ref[...])


def _matmul_res(a, w, h, *, tm, tn):
    t, k = a.shape
    n = w.shape[1]
    return pl.pallas_call(
        _matmul_res_kernel,
        grid=(t // tm, n // tn),
        in_specs=[pl.BlockSpec((tm, k), lambda i, j: (i, 0)),
                  pl.BlockSpec((k, tn), lambda i, j: (0, j)),
                  pl.BlockSpec((tm, tn), lambda i, j: (i, j))],
        out_specs=pl.BlockSpec((tm, tn), lambda i, j: (i, j)),
        out_shape=jax.ShapeDtypeStruct((t, n), F32),
        compiler_params=_cparams("parallel", "arbitrary"),
        name="matmul_res",
    )(a, w, h)


def _ffn_kernel(x_ref, nw_ref, wg_ref, wu_ref, wo_ref, o_ref, xn_ref):
    @pl.when(pl.program_id(1) == 0)
    def _():
        x = x_ref[...]
        xn_ref[...] = (x * _rms_scale(x) * nw_ref[...]).astype(BF16)
        o_ref[...] = x

    xn = xn_ref[...]
    g = _dot(xn, wg_ref[...])
    u = _dot(xn, wu_ref[...])
    act = (g * _sigmoid(g) * u).astype(BF16)
    o_ref[...] += _dot(act, wo_ref[...])


def _ffn(x, nw, w_in, w_out, *, tm, tf):
    t, d = x.shape
    f = w_out.shape[0]
    nf = f // tf
    return pl.pallas_call(
        _ffn_kernel,
        grid=(t // tm, nf),
        in_specs=[pl.BlockSpec((tm, d), lambda i, j: (i, 0), pipeline_mode=pl.Buffered(1)),
                  pl.BlockSpec((1, d), lambda i, j: (0, 0)),
                  pl.BlockSpec((d, tf), lambda i, j: (0, j)),
                  pl.BlockSpec((d, tf), lambda i, j: (0, nf + j)),
                  pl.BlockSpec((tf, d), lambda i, j: (j, 0))],
        out_specs=pl.BlockSpec((tm, d), lambda i, j: (i, 0)),
        out_shape=jax.ShapeDtypeStruct((t, d), F32),
        scratch_shapes=[pltpu.VMEM((tm, d), BF16)],
        compiler_params=_cparams("parallel", "arbitrary"),
        name="ffn",
    )(x, nw.reshape(1, d), w_in, w_in, w_out)


def _ple_kernel(h_ref, nw_ref, wg_ref, p_ref, wp_ref, fn_ref, o_ref, *, final):
    h = h_ref[...]
    xn = (h * _rms_scale(h) * nw_ref[...]).astype(BF16)
    gate = _sigmoid(_dot(xn, wg_ref[...]))
    y = h + gate * _dot(p_ref[...].astype(BF16), wp_ref[...])
    if final:
        y = y * _rms_scale(y) * fn_ref[...]
    o_ref[...] = y


def _ple(h, nw, wg, p, wp, fn, *, tm, final):
    t, d = h.shape
    kp = p.shape[1]
    return pl.pallas_call(
        functools.partial(_ple_kernel, final=final),
        grid=(t // tm,),
        in_specs=[pl.BlockSpec((tm, d), lambda i: (i, 0)),
                  pl.BlockSpec((1, d), lambda i: (0, 0)),
                  pl.BlockSpec((d, d), lambda i: (0, 0), pipeline_mode=pl.Buffered(1)),
                  pl.BlockSpec((tm, kp), lambda i: (i, 0)),
                  pl.BlockSpec((kp, d), lambda i: (0, 0), pipeline_mode=pl.Buffered(1)),
                  pl.BlockSpec((1, d), lambda i: (0, 0))],
        out_specs=pl.BlockSpec((tm, d), lambda i: (i, 0)),
        out_shape=jax.ShapeDtypeStruct((t, d), F32),
        compiler_params=_cparams("parallel"),
        name="ple",
    )(h, nw.reshape(1, d), wg, p, wp, fn.reshape(1, d))


def _gla_levels(c):
    out, b = [], c // 2
    while b >= 1:
        out.append(b)
        b //= 2
    return out


def _gla_consts(c):
    idx = np.arange(c)
    i, m = idx[:, None], idx[None, :]
    blocks = [(m <= i)]
    masks = []
    for b in _gla_levels(c):
        mid = (idx // (2 * b)) * (2 * b) + b - 1
        md = mid[:, None]
        blocks.append(np.where(i > md, (m > md) & (m <= i), (m > i) & (m <= md)))
        masks.append((i // (2 * b) == m // (2 * b)) & (i % (2 * b) >= b) & (m % (2 * b) < b))
    masks.append(i == m)
    fwd_w = np.concatenate([bl.astype(np.float32) for bl in blocks], axis=0)
    bwd_w = np.concatenate([bl.astype(np.float32)[::-1, ::-1] for bl in blocks], axis=0)
    ones = np.ones((V7X_SUBLANES, c), np.float32)
    wcum = np.stack([np.concatenate([bwd_w, ones], 0), np.concatenate([fwd_w, ones], 0)])
    fwd_m = np.stack([mk.astype(np.float32) for mk in masks])
    bwd_m = np.stack([mk.astype(np.float32)[::-1, ::-1] for mk in masks])
    mask = np.stack([bwd_m, fwd_m])
    return wcum, mask


def _gla_kernel(q_ref, k_ref, v_ref, og_ref, lr_ref, wgu_ref, bg_ref, wcum_ref, mask_ref,
                onorm_ref, o_ref, st_ref, obwd_ref, qd_ref, oa_ref, dl_ref, et_ref,
                *, c, nci, tb):
    d = pl.program_id(1)
    b = pl.program_id(2)
    nb = pl.num_programs(2)
    nlev = len(_gla_levels(c))

    @pl.when(b == 0)
    def _():
        st_ref[...] = jnp.zeros_like(st_ref)

    def prepare():
        chunks = range(nci)
        rows = [slice(ci * c, (ci + 1) * c) for ci in chunks]
        q = [q_ref[r, :] * (GLA_DK ** -0.5) for r in rows]
        k = [k_ref[r, :] for r in rows]
        vb = [v_ref[r, :].astype(BF16) for r in rows]
        gk = [_dot(lr_ref[r, :].astype(BF16), wgu_ref[...]) + bg_ref[...] for r in rows]
        g = [(jnp.minimum(x, 0.0) - jnp.log(1.0 + jnp.exp(-jnp.abs(x)))) * (LOG2_E / GLA_GATE_NORM) for x in gk]
        gp = [jnp.concatenate(_split3(x)[:2], axis=1) for x in g]
        cums = [_dot(wcum_ref[...], x) for x in gp]
        cums = [x[:, :GLA_DK] + x[:, GLA_DK:] for x in cums]
        cum = [x[0:c] for x in cums]
        tot = [x[(1 + nlev) * c:(1 + nlev) * c + V7X_SUBLANES] for x in cums]
        rest = [tt[0:1] - cc for tt, cc in zip(tot, cum)]
        for ci in chunks:
            et_ref[ci] = jnp.exp2(tot[ci])

        a = [jnp.sum(qq * kk, axis=-1, keepdims=True) * mask_ref[nlev] for qq, kk in zip(q, k)]
        for lv in range(nlev):
            e = [jnp.exp2(x[(1 + lv) * c:(2 + lv) * c]) for x in cums]
            s = [_dot_nt((qq * ee).astype(BF16), (kk * ee).astype(BF16)) for qq, kk, ee in zip(q, k, e)]
            a = [aa + ss * mask_ref[lv] for aa, ss in zip(a, s)]
        a = [aa.astype(BF16) for aa in a]

        kd = [(kk * jnp.exp2(rr)).astype(BF16) for kk, rr in zip(k, rest)]
        for ci in chunks:
            qd_ref[ci] = (q[ci] * jnp.exp2(cum[ci])).astype(BF16)
            oa_ref[ci] = _dot(a[ci], vb[ci])
            dl_ref[ci] = _dot_tn(vb[ci], kd[ci])

    def walk(order, blk, forward):
        for ci in order:
            rows = slice(ci * c, (ci + 1) * c)
            st = st_ref[...]
            o = _dot_nt(qd_ref[ci], st.astype(BF16)) + oa_ref[ci]
            st_ref[...] = st * et_ref[ci][0:1] + dl_ref[ci]
            orow = pl.ds(pl.multiple_of(blk * tb + ci * c, c), c)
            if forward:
                ot = o + obwd_ref[orow, :]
                y = ot * _rms_scale(ot) * onorm_ref[...]
                og = og_ref[rows, :]
                o_ref[rows, :] = (y * (og * _sigmoid(og))).astype(o_ref.dtype)
            else:
                obwd_ref[orow, :] = o

    prepare()

    @pl.when(d == 0)
    def _():
        walk(range(nci - 1, -1, -1), nb - 1 - b, False)

    @pl.when(d == 1)
    def _():
        walk(range(nci), b, True)


def _gla_scan(proj, wgu, bg, onorm, *, t):
    c = SCAN_CHUNK
    tb = min(t, 512)
    nb = t // tb
    nci = tb // c
    wcum, mask = _gla_consts(c)
    nlev = len(_gla_levels(c))
    qoff = 0
    koff = GLA_HEADS * GLA_DK // GLA_DK
    voff = 2 * GLA_HEADS * GLA_DK // GLA_DV
    goff = voff + GLA_HEADS
    lroff = (2 * GLA_HEADS * GLA_DK + 2 * GLA_HEADS * GLA_DV) // V7X_LANES

    def tblk(d, b):
        return d * b + (1 - d) * (nb - 1 - b)

    return pl.pallas_call(
        functools.partial(_gla_kernel, c=c, nci=nci, tb=tb),
        grid=(GLA_HEADS, 2, nb),
        in_specs=[
            pl.BlockSpec((tb, GLA_DK), lambda h, d, b: (tblk(d, b), qoff + h)),
            pl.BlockSpec((tb, GLA_DK), lambda h, d, b: (tblk(d, b), koff + h)),
            pl.BlockSpec((tb, GLA_DV), lambda h, d, b: (tblk(d, b), voff + h)),
            pl.BlockSpec((tb, GLA_DV), lambda h, d, b: (tblk(d, b), goff + h)),
            pl.BlockSpec((tb, V7X_LANES), lambda h, d, b: (tblk(d, b), lroff)),
            pl.BlockSpec((None, None, V7X_LANES, GLA_DK), lambda h, d, b: (d, h, 0, 0)),
            pl.BlockSpec((None, None, 1, GLA_DK), lambda h, d, b: (d, h, 0, 0)),
            pl.BlockSpec((None, wcum.shape[1], c), lambda h, d, b: (d, 0, 0)),
            pl.BlockSpec((None, nlev + 1, c, c), lambda h, d, b: (d, 0, 0, 0)),
            pl.BlockSpec((1, GLA_DV), lambda h, d, b: (0, 0)),
        ],
        out_specs=pl.BlockSpec((tb, GLA_DV), lambda h, d, b: (d * b, h)),
        out_shape=jax.ShapeDtypeStruct((t, GLA_HEADS * GLA_DV), BF16),
        scratch_shapes=[pltpu.VMEM((GLA_DV, GLA_DK), F32),
                        pltpu.VMEM((t, GLA_DV), F32),
                        pltpu.VMEM((nci, c, GLA_DK), BF16),
                        pltpu.VMEM((nci, c, GLA_DV), F32),
                        pltpu.VMEM((nci, GLA_DV, GLA_DK), F32),
                        pltpu.VMEM((nci, V7X_SUBLANES, GLA_DK), F32)],
        compiler_params=_cparams("arbitrary", "arbitrary", "arbitrary"),
        name="gla_scan",
    )(proj, proj, proj, proj, proj, wgu, bg,
      jnp.asarray(wcum, BF16), jnp.asarray(mask, F32),
      onorm.reshape(1, GLA_DV))


def _gla_gate_weights(w_gate_up, b_gate):
    w = w_gate_up.reshape(2, GLA_RANK, GLA_HEADS, GLA_DK).transpose(0, 2, 1, 3)
    z = jnp.zeros((GLA_HEADS, V7X_LANES, GLA_DK), F32)
    fwd = z.at[:, 0:GLA_RANK].set(w[0])
    bwd = z.at[:, GLA_RANK:2 * GLA_RANK].set(w[1])
    wgu = jnp.stack([bwd, fwd]).astype(BF16)
    bg = jnp.stack([b_gate[1], b_gate[0]]).reshape(2, GLA_HEADS, 1, GLA_DK)
    return wgu, bg


def _gdn_qkv_kernel(prev_ref, cur_ref, next_ref, nw_ref, w_ref, cw_ref, o_ref, xn_ref, *r_refs, tm, nq, nqk):
    i = pl.program_id(0)
    j = pl.program_id(1)
    halo = V7X_SUBLANES

    @pl.when(j == 0)
    def _():
        def normed(x):
            return (x * _rms_scale(x) * nw_ref[...]).astype(BF16)

        zero = jnp.zeros((halo, xn_ref.shape[1]), BF16)
        xn_ref[0:halo, :] = jnp.where(i > 0, normed(prev_ref[...]), zero)
        xn_ref[halo:halo + tm, :] = normed(cur_ref[...])
        xn_ref[halo + tm:2 * halo + tm, :] = jnp.where(i < pl.num_programs(0) - 1, normed(next_ref[...]), zero)

    is_qk = j < nqk
    qs = jnp.where(j < nq, GDN_DK ** -0.5, 1.0)
    n = PREP_ROWS + 2 * halo
    pw = V7X_MXU_COLS
    npieces = w_ref.shape[1] // pw

    nslab = tm // SLAB_ROWS
    edges = [0] + [(s + 1) * SLAB_ROWS + 2 * halo for s in range(nslab)]

    def project(p, s):
        rows = slice(edges[s], edges[s + 1])
        r_refs[p][rows, :] = _dot(xn_ref[rows, :], w_ref[:, p * pw:(p + 1) * pw])

    ng = n // halo
    sub3 = lax.broadcasted_iota(jnp.int32, (ng - 2, halo, GDN_DK), 1)

    def conv(p, s):
        for r0 in range(s * SLAB_ROWS, (s + 1) * SLAB_ROWS, PREP_ROWS):
            for hh in range(pw // GDN_DK):
                cols = slice(p * pw + hh * GDN_DK, p * pw + (hh + 1) * GDN_DK)
                xs = r_refs[p][r0:r0 + n, hh * GDN_DK:(hh + 1) * GDN_DK]
                x3 = xs.reshape(ng, halo, GDN_DK)
                acc = x3[1:ng - 1] * cw_ref[CONV_W // 2:CONV_W // 2 + 1, cols]
                for w in range(CONV_W):
                    sft = w - CONV_W // 2
                    if sft == 0:
                        continue
                    rot = pltpu.roll(x3, (-sft) % halo, axis=1)
                    if sft > 0:
                        sh = jnp.where(sub3 < halo - sft, rot[1:ng - 1], rot[2:ng])
                    else:
                        sh = jnp.where(sub3 >= -sft, rot[1:ng - 1], rot[0:ng - 2])
                    acc = acc + sh * cw_ref[w:w + 1, cols]
                acc = acc.reshape(PREP_ROWS, GDN_DK)
                half = 0.5 * acc
                y = half + half * jnp.tanh(half)
                ss = jnp.sum(y * y, axis=-1, keepdims=True)
                scale = jnp.where(is_qk, lax.rsqrt(ss + EPS) * qs, 1.0)
                o_ref[r0:r0 + PREP_ROWS, cols] = (y * scale).astype(o_ref.dtype)

    for s in range(nslab):
        project(0, s)
    for p in range(1, npieces):
        for s in range(nslab):
            project(p, s)
            conv(p - 1, s)
    for s in range(nslab):
        conv(npieces - 1, s)


def _gdn_qkv(x, nw, w, conv_w, *, tm, tn):
    t, d = x.shape
    ncols = conv_w.shape[1]
    hb = tm // V7X_SUBLANES
    nt = t // tm
    nq = GDN_QK_HEADS * GDN_DK // tn
    return pl.pallas_call(
        functools.partial(_gdn_qkv_kernel, tm=tm, nq=nq, nqk=2 * nq),
        grid=(nt, ncols // tn),
        in_specs=[
            pl.BlockSpec((V7X_SUBLANES, d), lambda i, j: (jnp.maximum(i * hb - 1, 0), 0)),
            pl.BlockSpec((tm, d), lambda i, j: (i, 0)),
            pl.BlockSpec((V7X_SUBLANES, d), lambda i, j: (jnp.minimum((i + 1) * hb, nt * hb - 1), 0)),
            pl.BlockSpec((1, d), lambda i, j: (0, 0)),
            pl.BlockSpec((d, tn), lambda i, j: (0, j)),
            pl.BlockSpec((CONV_W, tn), lambda i, j: (0, j)),
        ],
        out_specs=pl.BlockSpec((tm, tn), lambda i, j: (i, j)),
        out_shape=jax.ShapeDtypeStruct((t, ncols), BF16),
        scratch_shapes=[pltpu.VMEM((tm + 2 * V7X_SUBLANES, d), BF16)]
        + [pltpu.VMEM((tm + 2 * V7X_SUBLANES, V7X_MXU_COLS), F32) for _ in range(tn // V7X_MXU_COLS)],
        compiler_params=_cparams("parallel", "arbitrary"),
        name="gdn_qkv",
    )(x, x, x, nw.reshape(1, d), w, conv_w)


def _gdn_consts(c):
    idx = np.arange(c)
    i, m = idx[:, None], idx[None, :]
    u_f = (i <= m)
    uv_f = np.concatenate([u_f, ~u_f], axis=1).astype(np.float32)
    u_b = (i >= m)
    uv_b = np.concatenate([u_b, ~u_b], axis=1).astype(np.float32)
    uv = np.stack([uv_b, uv_f])

    def pair(b):
        return (i // (2 * b) == m // (2 * b)) & (i // b != m // b)

    lower, strict = (m <= i), (m < i)
    blk = (i // 16 == m // 16)
    levels = []
    b = 16
    while 2 * b <= c:
        levels.append(pair(b))
        b *= 2
    fwd = [lower, strict, blk & strict] + [p & strict for p in levels]
    mask_f = np.stack([x.astype(np.float32) for x in fwd])
    mask_b = np.stack([x.astype(np.float32)[::-1, ::-1] for x in fwd])
    mask = np.stack([mask_b, mask_f])
    return uv, mask, np.eye(c, dtype=np.float32), len(levels)


def _gdn_kernel(q_ref, k_ref, v_ref, z_ref, gt_ref, alog_ref, dtb_ref, uv_ref, mask_ref,
                eye_ref, onorm_ref, o_ref, s_ref, obwd_ref, mq_ref, bo_ref, et_ref,
                *, c, nci, tb, nb, nlev):
    hq = pl.program_id(0)
    d = pl.program_id(1)
    b = pl.program_id(2)

    @pl.when((hq == 0) & (d == 0) & (b == 0))
    def _():
        obwd_ref[...] = jnp.zeros_like(obwd_ref)
        s_ref[...] = jnp.zeros_like(s_ref)
        mq_ref[...] = jnp.zeros_like(mq_ref)
        bo_ref[...] = jnp.zeros_like(bo_ref)
        et_ref[...] = jnp.zeros_like(et_ref)

    row8 = lax.broadcasted_iota(jnp.int32, (V7X_SUBLANES, c), 0)

    def prepare(slot, chunk_ids):
        chunks = range(len(chunk_ids))
        units = [(ci, hh) for ci in chunks for hh in range(2)]
        rows = [slice(ca * c, (ca + 1) * c) for ca in chunk_ids]
        gts = [gt_ref[ca] for ca in chunk_ids]
        beta8 = [_sigmoid(gt) for gt in gts]
        g8 = [-jnp.exp(alog_ref[...]) * _softplus(gt + dtb_ref[...]) for gt in gts]
        gp = [jnp.concatenate(_split3(g), axis=0) for g in g8]
        qb = [q_ref[r, :] for r in rows]
        kb = [k_ref[r, :] for r in rows]
        qf = [q.astype(F32) for q in qb]
        kf = [k.astype(F32) for k in kb]
        cl = [_dot(g, uv_ref[...]) for g in gp]
        kk = [_dot_nt(k, k) for k in kb]
        qk = [_dot_nt(q, k) for q, k in zip(qb, kb)]
        kt = [k.T for k in kf]
        cl = [x[0:8] + x[8:16] + x[16:24] for x in cl]
        cum8 = [x[:, :c] for x in cl]
        rest8 = [x[:, c:] for x in cl]
        cols = [jnp.concatenate([jnp.where(row8 < 2, bt, cm), jnp.zeros((c - V7X_SUBLANES, c), F32)],
                                axis=0).T for bt, cm in zip(beta8, cum8)]
        for ci in chunks:
            et_ref[slot, chunk_ids[ci]] = jnp.exp(cum8[ci] + rest8[ci])
        yield

        beta_c = [cols[ci][:, hh:hh + 1] for ci, hh in units]
        cum_c = [cols[ci][:, 2 + hh:3 + hh] for ci, hh in units]
        dec = [jnp.exp(jnp.minimum(cc - cum8[ci][2 + hh:3 + hh, :], 0.0))
               for cc, (ci, hh) in zip(cum_c, units)]
        lm = [kk[ci] * dc * bc * mask_ref[1] for dc, bc, (ci, hh) in zip(dec, beta_c, units)]
        at = [(qk[ci] * dc * mask_ref[0]).astype(BF16) for dc, (ci, hh) in zip(dec, units)]

        n1 = [-(l * mask_ref[2]) for l in lm]
        x = [eye_ref[...] + n for n in n1]
        p = [n.astype(BF16) for n in n1]
        yield
        for _ in range(3):
            p = [_dot(pp, pp).astype(BF16) for pp in p]
            x = [xx + _dot(xx.astype(BF16), pp) for xx, pp in zip(x, p)]
            yield
        for lv in range(nlev):
            xb = [xx.astype(BF16) for xx in x]
            t1 = [_dot((l * mask_ref[3 + lv]).astype(BF16), bb).astype(BF16) for l, bb in zip(lm, xb)]
            x = [xx - _dot(bb, tt) for xx, bb, tt in zip(x, xb, t1)]
            yield

        e_c = [jnp.exp(cc) for cc in cum_c]
        rhs = [jnp.concatenate([v_ref[rows[ci], hh * GDN_DV:(hh + 1) * GDN_DV].astype(F32) * bc,
                                kf[ci] * (bc * ec)], axis=1).astype(BF16)
               for bc, ec, (ci, hh) in zip(beta_c, e_c, units)]
        uw = [_dot(xx.astype(BF16), r).astype(BF16) for xx, r in zip(x, rhs)]
        yield
        kdt = [(kt[ci] * jnp.exp(rest8[ci][2 + hh:3 + hh, :])).astype(BF16) for ci, hh in units]
        pr = [_dot(jnp.concatenate([kd, a], axis=0), w) for kd, a, w in zip(kdt, at, uw)]
        for n, (ci, hh) in enumerate(units):
            m = -pr[n][:GDN_DK, GDN_DV:]
            qm = qf[ci] * e_c[n] - pr[n][GDN_DK:, GDN_DV:]
            mq_ref[slot, chunk_ids[ci], hh] = jnp.concatenate([m, qm], axis=0).astype(BF16)
            bo_ref[slot, chunk_ids[ci], hh] = pr[n][:, :GDN_DV]

    forward = d == 1

    def walk_step(i, slot, wb):
        ci = jnp.where(forward, i, nci - 1 - i)
        r0 = pl.multiple_of(ci * c, c)
        rows = pl.ds(r0, c)
        orow = pl.ds(pl.multiple_of(wb * tb + r0, c), c)
        et = et_ref[slot, ci]
        for hh in range(2):
            s = s_ref[hh]
            r = _dot(mq_ref[slot, ci, hh], s.astype(BF16)) + bo_ref[slot, ci, hh]
            s_ref[hh] = s * et[2 + hh:3 + hh, :] + r[:GDN_DK]
            o = r[GDN_DK:]
            ocol = slice(hh * GDN_DV, (hh + 1) * GDN_DV)
            parked = obwd_ref[orow, ocol]
            ot = o + jnp.where(forward, parked, 0.0)
            obwd_ref[orow, ocol] = jnp.where(forward, parked, o)
            y = ot * _rms_scale(ot) * onorm_ref[...]
            z = z_ref[rows, ocol]
            o_ref[rows, ocol] = (y * (z * _sigmoid(z))).astype(o_ref.dtype)

    def time_block(step_b):
        return jnp.where(forward, step_b, nb - 1 - step_b)

    @pl.when(b < nb)
    def _():
        wb = time_block(jnp.maximum(b - 1, 0))
        ngrp = -(-nci // UNIT_CHUNKS)
        per = nci // ngrp
        for g in range(ngrp):
            stages = prepare(b % 2, list(range(g * per, (g + 1) * per)))
            for i in range(g * per, (g + 1) * per):
                next(stages)
                walk_step(i, (b + 1) % 2, wb)
            for _ in stages:
                pass

    @pl.when(b == 0)
    def _():
        s_ref[...] = jnp.zeros_like(s_ref)

    @pl.when(b == nb)
    def _():
        for i in range(nci):
            walk_step(i, (nb - 1) % 2, time_block(nb - 1))


def _gdn_scan(qkv, proj, gt, alog, dtb, onorm, *, t, tb):
    c = SCAN_CHUNK
    nb = t // tb
    nci = tb // c
    uv, mask, eye, nlev = _gdn_consts(c)
    hw = 2 * GDN_DV
    voff = 2 * GDN_QK_HEADS * GDN_DK // hw

    def tblk(d, b):
        return d * b + (1 - d) * (nb - 1 - b)

    def prep_blk(d, b):
        return tblk(d, jnp.minimum(b, nb - 1))

    def walk_blk(d, b):
        return tblk(d, jnp.maximum(b - 1, 0))

    return pl.pallas_call(
        functools.partial(_gdn_kernel, c=c, nci=nci, tb=tb, nb=nb, nlev=nlev),
        grid=(GDN_QK_HEADS, 2, nb + 1),
        in_specs=[
            pl.BlockSpec((tb, GDN_DK), lambda h, d, b: (prep_blk(d, b), h)),
            pl.BlockSpec((tb, GDN_DK), lambda h, d, b: (prep_blk(d, b), GDN_QK_HEADS + h)),
            pl.BlockSpec((tb, hw), lambda h, d, b: (prep_blk(d, b), voff + h)),
            pl.BlockSpec((tb, hw), lambda h, d, b: (walk_blk(d, b), h)),
            pl.BlockSpec((None, None, nci, V7X_SUBLANES, c), lambda h, d, b: (d, h, prep_blk(d, b), 0, 0)),
            pl.BlockSpec((None, None, V7X_SUBLANES, c), lambda h, d, b: (d, h, 0, 0)),
            pl.BlockSpec((None, None, V7X_SUBLANES, c), lambda h, d, b: (d, h, 0, 0)),
            pl.BlockSpec((None, c, 2 * c), lambda h, d, b: (d, 0, 0)),
            pl.BlockSpec((None, mask.shape[1], c, c), lambda h, d, b: (d, 0, 0, 0)),
            pl.BlockSpec((c, c), lambda h, d, b: (0, 0)),
            pl.BlockSpec((1, GDN_DV), lambda h, d, b: (0, 0)),
        ],
        out_specs=pl.BlockSpec((tb, hw), lambda h, d, b: (d * jnp.maximum(b - 1, 0), h)),
        out_shape=jax.ShapeDtypeStruct((t, 2 * GDN_QK_HEADS * GDN_DV), BF16),
        scratch_shapes=[pltpu.VMEM((2, GDN_DK, GDN_DV), F32),
                        pltpu.VMEM((t, hw), F32),
                        pltpu.VMEM((2, nci, 2, GDN_DK + c, GDN_DK), BF16),
                        pltpu.VMEM((2, nci, 2, GDN_DK + c, GDN_DV), F32),
                        pltpu.VMEM((2, nci, V7X_SUBLANES, c), F32)],
        compiler_params=_cparams("arbitrary", "arbitrary", "arbitrary"),
        name="gdn_scan",
    )(qkv, qkv, qkv, proj, gt, alog, dtb,
      jnp.asarray(uv, BF16), jnp.asarray(mask, F32), jnp.asarray(eye, F32),
      onorm.reshape(1, GDN_DV))


def _gdn_gate_layout(ba, a_log, dt_bias, *, t):
    c = SCAN_CHUNK
    g = ba.reshape(t, 2, 2, GDN_QK_HEADS, 2)
    g = g.transpose(2, 3, 1, 4, 0).reshape(2, GDN_QK_HEADS, 4, t)
    g = jnp.concatenate([g, jnp.zeros_like(g)], axis=2)
    g = g.reshape(2, GDN_QK_HEADS, V7X_SUBLANES, t // c, c).transpose(0, 1, 3, 2, 4)
    g = jnp.stack([g[1], g[0]])

    def rows(pv):
        pv = pv.reshape(2, GDN_QK_HEADS, 2)
        z = jnp.zeros_like(pv)
        r = jnp.concatenate([z, pv, z, z], axis=2)
        r = jnp.broadcast_to(r[..., None], (2, GDN_QK_HEADS, V7X_SUBLANES, c))
        return jnp.stack([r[1], r[0]])

    return g, rows(a_log.astype(F32)), rows(dt_bias.astype(F32))


def _pad_cols(w, n):
    return jnp.concatenate([w, jnp.zeros((w.shape[0], n - w.shape[1]), w.dtype)], axis=1)


def _proj_tile(n):
    best = None
    for tn in range(PROJ_TN_MAX, 3 * V7X_MXU_COLS, -V7X_MXU_COLS):
        n_pad = -(-n // tn) * tn
        if best is None or n_pad < best[1]:
            best = (tn, n_pad)
    return best


def kernel(x, p, mixer_norm, gla_w_in, gla_w_gate_up, gla_b_gate, gla_out_norm, gla_w_out,
           gdn_w_in, gdn_conv, gdn_a_log, gdn_dt_bias, gdn_out_norm, gdn_w_out,
           ffn_norm, ffn_w_in, ffn_w_out, ple_norm, ple_w_gate, ple_w_proj, final_norm):
    _, t, dm = x.shape
    depth = mixer_norm.shape[0]
    h = x.reshape(t, dm)
    tm = min(t, 1024)
    tm_ple = min(t, 512)
    tn_out = 1024
    tf = 512
    tn_qkv = 1024
    tb_gdn = min(t, 2048)
    for i in range(depth):
        j = i // 2
        if i % 2 == 0:
            tn, n_pad = _proj_tile(gla_w_in.shape[2])
            w_in = _pad_cols(gla_w_in[j].astype(BF16), n_pad)
            proj = _norm_matmul(h, mixer_norm[i], w_in, tm=tm, tn=tn)
            wgu, bg = _gla_gate_weights(gla_w_gate_up[j], gla_b_gate[j])
            o = _gla_scan(proj, wgu, bg, gla_out_norm[j], t=t)
            h = _matmul_res(o, gla_w_out[j].astype(BF16), h, tm=tm, tn=tn_out)
        else:
            nconv = gdn_conv.shape[2]
            vd = 2 * GDN_QK_HEADS * GDN_DV
            w_bf = gdn_w_in[j].astype(BF16)
            qkv = _gdn_qkv(h, mixer_norm[i], w_bf[:, :nconv], gdn_conv[j], tm=tm, tn=tn_qkv)
            tn, n_pad = _proj_tile(gdn_w_in.shape[2] - nconv)
            zba = _norm_matmul(h, mixer_norm[i], _pad_cols(w_bf[:, nconv:], n_pad), tm=tm, tn=tn)
            gt, alog, dtb = _gdn_gate_layout(zba[:, vd:vd + 4 * 2 * GDN_QK_HEADS], gdn_a_log[j], gdn_dt_bias[j], t=t)
            o = _gdn_scan(qkv, zba, gt, alog, dtb, gdn_out_norm[j], t=t, tb=tb_gdn)
            h = _matmul_res(o, gdn_w_out[j].astype(BF16), h, tm=tm, tn=tn_out)
        h = _ffn(h, ffn_norm[i], ffn_w_in[i].astype(BF16), ffn_w_out[i].astype(BF16), tm=tm, tf=tf)
        h = _ple(h, ple_norm[i], ple_w_gate[i].astype(BF16), p[i].reshape(t, -1),
                 ple_w_proj[i].astype(BF16), final_norm, tm=tm_ple, final=(i == depth - 1))
    return h.reshape(x.shape)
```

```python
import functools

import numpy as np
import jax
import jax.numpy as jnp
from jax import lax
from jax.experimental import pallas as pl
from jax.experimental.pallas import tpu as pltpu

F32 = jnp.float32
BF16 = jnp.bfloat16
EPS = 1e-6

V7X_LANES = 128
V7X_SUBLANES = 8
V7X_VMEM_LIMIT_BYTES = 56 * 1024 * 1024

GLA_HEADS = 4
GLA_DK = 256
GLA_DV = 512
GLA_RANK = 16
GLA_GATE_NORM = 16.0
GDN_QK_HEADS = 16
GDN_DK = 128
GDN_DV = 128
CONV_W = 5
SCAN_CHUNK = 128
LOG2_E = 1.4426950408889634
UNIT_CHUNKS = 8
SLAB_ROWS = 256
PREP_ROWS = 256
V7X_MXU_COLS = 256
PROJ_TN_MAX = 1792


def _cparams(*sem):
    return pltpu.CompilerParams(dimension_semantics=sem,
                                vmem_limit_bytes=V7X_VMEM_LIMIT_BYTES)


def _dot(a, b):
    return jnp.dot(a, b, preferred_element_type=F32)


def _dot_nt(a, b):
    return lax.dot_general(a, b, (((1,), (1,)), ((), ())), preferred_element_type=F32)


def _dot_tn(a, b):
    return lax.dot_general(a, b, (((0,), (0,)), ((), ())), preferred_element_type=F32)


def _rms_scale(x):
    return lax.rsqrt(jnp.mean(x * x, axis=-1, keepdims=True) + EPS)


def _sigmoid(x):
    return 1.0 / (1.0 + jnp.exp(-x))


def _softplus(x):
    return jnp.maximum(x, 0.0) + jnp.log(1.0 + jnp.exp(-jnp.abs(x)))


def _split3(x):
    p1 = x.astype(BF16)
    r = x - p1.astype(F32)
    p2 = r.astype(BF16)
    p3 = (r - p2.astype(F32)).astype(BF16)
    return p1, p2, p3


def _norm_matmul_kernel(x_ref, nw_ref, w_ref, o_ref, xn_ref):
    @pl.when(pl.program_id(1) == 0)
    def _():
        x = x_ref[...]
        xn_ref[...] = (x * _rms_scale(x) * nw_ref[...]).astype(BF16)

    o_ref[...] = _dot(xn_ref[...], w_ref[...]).astype(o_ref.dtype)


def _norm_matmul(x, nw, w, *, tm, tn):
    t, d = x.shape
    n = w.shape[1]
    return pl.pallas_call(
        _norm_matmul_kernel,
        grid=(t // tm, n // tn),
        in_specs=[pl.BlockSpec((tm, d), lambda i, j: (i, 0)),
                  pl.BlockSpec((1, d), lambda i, j: (0, 0)),
                  pl.BlockSpec((d, tn), lambda i, j: (0, j))],
        out_specs=pl.BlockSpec((tm, tn), lambda i, j: (i, j)),
        out_shape=jax.ShapeDtypeStruct((t, n), F32),
        scratch_shapes=[pltpu.VMEM((tm, d), BF16)],
        compiler_params=_cparams("parallel", "arbitrary"),
        name="norm_matmul",
    )(x, nw.reshape(1, d), w)


def _matmul_res_kernel(a_ref, w_ref, h_ref, o_ref):
    o_ref[...] = h_ref[...] + _dot(a_ref[...], w_ref[...])


def _matmul_res(a, w, h, *, tm, tn):
    t, k = a.shape
    n = w.shape[1]
    return pl.pallas_call(
        _matmul_res_kernel,
        grid=(t // tm, n // tn),
        in_specs=[pl.BlockSpec((tm, k), lambda i, j: (i, 0)),
                  pl.BlockSpec((k, tn), lambda i, j: (0, j)),
                  pl.BlockSpec((tm, tn), lambda i, j: (i, j))],
        out_specs=pl.BlockSpec((tm, tn), lambda i, j: (i, j)),
        out_shape=jax.ShapeDtypeStruct((t, n), F32),
        compiler_params=_cparams("parallel", "arbitrary"),
        name="matmul_res",
    )(a, w, h)


def _ffn_kernel(x_ref, nw_ref, wg_ref, wu_ref, wo_ref, o_ref, xn_ref):
    @pl.when(pl.program_id(1) == 0)
    def _():
        x = x_ref[...]
        xn_ref[...] = (x * _rms_scale(x) * nw_ref[...]).astype(BF16)
        o_ref[...] = x

    xn = xn_ref[...]
    g = _dot(xn, wg_ref[...])
    u = _dot(xn, wu_ref[...])
    act = (g * _sigmoid(g) * u).astype(BF16)
    o_ref[...] += _dot(act, wo_ref[...])


def _ffn(x, nw, w_in, w_out, *, tm, tf):
    t, d = x.shape
    f = w_out.shape[0]
    nf = f // tf
    return pl.pallas_call(
        _ffn_kernel,
        grid=(t // tm, nf),
        in_specs=[pl.BlockSpec((tm, d), lambda i, j: (i, 0)),
                  pl.BlockSpec((1, d), lambda i, j: (0, 0)),
                  pl.BlockSpec((d, tf), lambda i, j: (0, j)),
                  pl.BlockSpec((d, tf), lambda i, j: (0, nf + j)),
                  pl.BlockSpec((tf, d), lambda i, j: (j, 0))],
        out_specs=pl.BlockSpec((tm, d), lambda i, j: (i, 0)),
        out_shape=jax.ShapeDtypeStruct((t, d), F32),
        scratch_shapes=[pltpu.VMEM((tm, d), BF16)],
        compiler_params=_cparams("parallel", "arbitrary"),
        name="ffn",
    )(x, nw.reshape(1, d), w_in, w_in, w_out)


def _ple_kernel(h_ref, nw_ref, wg_ref, p_ref, wp_ref, fn_ref, o_ref, *, final):
    h = h_ref[...]
    xn = (h * _rms_scale(h) * nw_ref[...]).astype(BF16)
    gate = _sigmoid(_dot(xn, wg_ref[...]))
    y = h + gate * _dot(p_ref[...].astype(BF16), wp_ref[...])
    if final:
        y = y * _rms_scale(y) * fn_ref[...]
    o_ref[...] = y


def _ple(h, nw, wg, p, wp, fn, *, tm, final):
    t, d = h.shape
    kp = p.shape[1]
    return pl.pallas_call(
        functools.partial(_ple_kernel, final=final),
        grid=(t // tm,),
        in_specs=[pl.BlockSpec((tm, d), lambda i: (i, 0)),
                  pl.BlockSpec((1, d), lambda i: (0, 0)),
                  pl.BlockSpec((d, d), lambda i: (0, 0), pipeline_mode=pl.Buffered(1)),
                  pl.BlockSpec((tm, kp), lambda i: (i, 0)),
                  pl.BlockSpec((kp, d), lambda i: (0, 0), pipeline_mode=pl.Buffered(1)),
                  pl.BlockSpec((1, d), lambda i: (0, 0))],
        out_specs=pl.BlockSpec((tm, d), lambda i: (i, 0)),
        out_shape=jax.ShapeDtypeStruct((t, d), F32),
        compiler_params=_cparams("parallel"),
        name="ple",
    )(h, nw.reshape(1, d), wg, p, wp, fn.reshape(1, d))


def _gla_levels(c):
    out, b = [], c // 2
    while b >= 1:
        out.append(b)
        b //= 2
    return out


def _gla_consts(c):
    idx = np.arange(c)
    i, m = idx[:, None], idx[None, :]
    blocks = [(m <= i)]
    masks = []
    for b in _gla_levels(c):
        mid = (idx // (2 * b)) * (2 * b) + b - 1
        md = mid[:, None]
        blocks.append(np.where(i > md, (m > md) & (m <= i), (m > i) & (m <= md)))
        masks.append((i // (2 * b) == m // (2 * b)) & (i % (2 * b) >= b) & (m % (2 * b) < b))
    masks.append(i == m)
    fwd_w = np.concatenate([bl.astype(np.float32) for bl in blocks], axis=0)
    bwd_w = np.concatenate([bl.astype(np.float32)[::-1, ::-1] for bl in blocks], axis=0)
    ones = np.ones((V7X_SUBLANES, c), np.float32)
    wcum = np.stack([np.concatenate([bwd_w, ones], 0), np.concatenate([fwd_w, ones], 0)])
    wcum = np.concatenate([wcum, wcum], axis=2)
    fwd_m = np.stack([mk.astype(np.float32) for mk in masks])
    bwd_m = np.stack([mk.astype(np.float32)[::-1, ::-1] for mk in masks])
    mask = np.stack([bwd_m, fwd_m])
    return wcum, mask


def _gla_kernel(q_ref, k_ref, v_ref, og_ref, lr_ref, wgu_ref, bg_ref, wcum_ref, mask_ref,
                onorm_ref, o_ref, st_ref, obwd_ref, qd_ref, oa_ref, dl_ref, et_ref,
                *, c, nci, tb):
    d = pl.program_id(1)
    b = pl.program_id(2)
    nb = pl.num_programs(2)
    nlev = len(_gla_levels(c))

    @pl.when(b == 0)
    def _():
        st_ref[...] = jnp.zeros_like(st_ref)

    def prepare():
        chunks = range(nci)
        rows = [slice(ci * c, (ci + 1) * c) for ci in chunks]
        q = [q_ref[r, :] * (GLA_DK ** -0.5) for r in rows]
        k = [k_ref[r, :] for r in rows]
        vb = [v_ref[r, :].astype(BF16) for r in rows]
        gk = [_dot(lr_ref[r, :].astype(BF16), wgu_ref[...]) + bg_ref[...] for r in rows]
        g = [(jnp.minimum(x, 0.0) - jnp.log(1.0 + jnp.exp(-jnp.abs(x)))) * (LOG2_E / GLA_GATE_NORM) for x in gk]
        gp = [jnp.concatenate(_split3(x)[:2], axis=0) for x in g]
        cums = [_dot(wcum_ref[...], x) for x in gp]
        cum = [x[0:c] for x in cums]
        tot = [x[(1 + nlev) * c:(1 + nlev) * c + V7X_SUBLANES] for x in cums]
        rest = [tt[0:1] - cc for tt, cc in zip(tot, cum)]
        for ci in chunks:
            et_ref[ci] = jnp.exp2(tot[ci])

        a = [jnp.sum(qq * kk, axis=-1, keepdims=True) * mask_ref[nlev] for qq, kk in zip(q, k)]
        for lv in range(nlev):
            e = [jnp.exp2(x[(1 + lv) * c:(2 + lv) * c]) for x in cums]
            s = [_dot_nt((qq * ee).astype(BF16), (kk * ee).astype(BF16)) for qq, kk, ee in zip(q, k, e)]
            a = [aa + ss * mask_ref[lv] for aa, ss in zip(a, s)]
        a = [aa.astype(BF16) for aa in a]

        kd = [(kk * jnp.exp2(rr)).astype(BF16) for kk, rr in zip(k, rest)]
        for ci in chunks:
            qd_ref[ci] = (q[ci] * jnp.exp2(cum[ci])).astype(BF16)
            oa_ref[ci] = _dot(a[ci], vb[ci])
            dl_ref[ci] = _dot_tn(vb[ci], kd[ci])

    def walk(order, blk, forward):
        for ci in order:
            rows = slice(ci * c, (ci + 1) * c)
            st = st_ref[...]
            o = _dot_nt(qd_ref[ci], st.astype(BF16)) + oa_ref[ci]
            st_ref[...] = st * et_ref[ci][0:1] + dl_ref[ci]
            orow = pl.ds(pl.multiple_of(blk * tb + ci * c, c), c)
            if forward:
                ot = o + obwd_ref[orow, :]
                y = ot * _rms_scale(ot) * onorm_ref[...]
                og = og_ref[rows, :]
                o_ref[rows, :] = (y * (og * _sigmoid(og))).astype(o_ref.dtype)
            else:
                obwd_ref[orow, :] = o

    prepare()

    @pl.when(d == 0)
    def _():
        walk(range(nci - 1, -1, -1), nb - 1 - b, False)

    @pl.when(d == 1)
    def _():
        walk(range(nci), b, True)


def _gla_scan(proj, wgu, bg, onorm, *, t):
    c = SCAN_CHUNK
    tb = min(t, 512)
    nb = t // tb
    nci = tb // c
    wcum, mask = _gla_consts(c)
    nlev = len(_gla_levels(c))
    qoff = 0
    koff = GLA_HEADS * GLA_DK // GLA_DK
    voff = 2 * GLA_HEADS * GLA_DK // GLA_DV
    goff = voff + GLA_HEADS
    lroff = (2 * GLA_HEADS * GLA_DK + 2 * GLA_HEADS * GLA_DV) // V7X_LANES

    def tblk(d, b):
        return d * b + (1 - d) * (nb - 1 - b)

    return pl.pallas_call(
        functools.partial(_gla_kernel, c=c, nci=nci, tb=tb),
        grid=(GLA_HEADS, 2, nb),
        in_specs=[
            pl.BlockSpec((tb, GLA_DK), lambda h, d, b: (tblk(d, b), qoff + h)),
            pl.BlockSpec((tb, GLA_DK), lambda h, d, b: (tblk(d, b), koff + h)),
            pl.BlockSpec((tb, GLA_DV), lambda h, d, b: (tblk(d, b), voff + h)),
            pl.BlockSpec((tb, GLA_DV), lambda h, d, b: (tblk(d, b), goff + h)),
            pl.BlockSpec((tb, V7X_LANES), lambda h, d, b: (tblk(d, b), lroff)),
            pl.BlockSpec((None, None, V7X_LANES, GLA_DK), lambda h, d, b: (d, h, 0, 0)),
            pl.BlockSpec((None, None, 1, GLA_DK), lambda h, d, b: (d, h, 0, 0)),
            pl.BlockSpec((None, wcum.shape[1], 2 * c), lambda h, d, b: (d, 0, 0)),
            pl.BlockSpec((None, nlev + 1, c, c), lambda h, d, b: (d, 0, 0, 0)),
            pl.BlockSpec((1, GLA_DV), lambda h, d, b: (0, 0)),
        ],
        out_specs=pl.BlockSpec((tb, GLA_DV), lambda h, d, b: (d * b, h)),
        out_shape=jax.ShapeDtypeStruct((t, GLA_HEADS * GLA_DV), BF16),
        scratch_shapes=[pltpu.VMEM((GLA_DV, GLA_DK), F32),
                        pltpu.VMEM((t, GLA_DV), F32),
                        pltpu.VMEM((nci, c, GLA_DK), BF16),
                        pltpu.VMEM((nci, c, GLA_DV), F32),
                        pltpu.VMEM((nci, GLA_DV, GLA_DK), F32),
                        pltpu.VMEM((nci, V7X_SUBLANES, GLA_DK), F32)],
        compiler_params=_cparams("arbitrary", "arbitrary", "arbitrary"),
        name="gla_scan",
    )(proj, proj, proj, proj, proj, wgu, bg,
      jnp.asarray(wcum, BF16), jnp.asarray(mask, F32),
      onorm.reshape(1, GLA_DV))


def _gla_gate_weights(w_gate_up, b_gate):
    w = w_gate_up.reshape(2, GLA_RANK, GLA_HEADS, GLA_DK).transpose(0, 2, 1, 3)
    z = jnp.zeros((GLA_HEADS, V7X_LANES, GLA_DK), F32)
    fwd = z.at[:, 0:GLA_RANK].set(w[0])
    bwd = z.at[:, GLA_RANK:2 * GLA_RANK].set(w[1])
    wgu = jnp.stack([bwd, fwd]).astype(BF16)
    bg = jnp.stack([b_gate[1], b_gate[0]]).reshape(2, GLA_HEADS, 1, GLA_DK)
    return wgu, bg


def _gdn_qkv_kernel(prev_ref, cur_ref, next_ref, nw_ref, w_ref, cw_ref, o_ref, xn_ref, *r_refs, tm, nq, nqk):
    i = pl.program_id(0)
    j = pl.program_id(1)
    halo = V7X_SUBLANES

    @pl.when(j == 0)
    def _():
        def normed(x):
            return (x * _rms_scale(x) * nw_ref[...]).astype(BF16)

        zero = jnp.zeros((halo, xn_ref.shape[1]), BF16)
        xn_ref[0:halo, :] = jnp.where(i > 0, normed(prev_ref[...]), zero)
        xn_ref[halo:halo + tm, :] = normed(cur_ref[...])
        xn_ref[halo + tm:2 * halo + tm, :] = jnp.where(i < pl.num_programs(0) - 1, normed(next_ref[...]), zero)

    is_qk = j < nqk
    qs = jnp.where(j < nq, GDN_DK ** -0.5, 1.0)
    n = PREP_ROWS + 2 * halo
    pw = V7X_MXU_COLS
    npieces = w_ref.shape[1] // pw

    nslab = tm // SLAB_ROWS
    edges = [0] + [(s + 1) * SLAB_ROWS + 2 * halo for s in range(nslab)]

    def project(p, s):
        rows = slice(edges[s], edges[s + 1])
        r_refs[p][rows, :] = _dot(xn_ref[rows, :], w_ref[:, p * pw:(p + 1) * pw])

    ng = n // halo
    sub3 = lax.broadcasted_iota(jnp.int32, (ng - 2, halo, GDN_DK), 1)

    def conv(p, s):
        for r0 in range(s * SLAB_ROWS, (s + 1) * SLAB_ROWS, PREP_ROWS):
            for hh in range(pw // GDN_DK):
                cols = slice(p * pw + hh * GDN_DK, p * pw + (hh + 1) * GDN_DK)
                xs = r_refs[p][r0:r0 + n, hh * GDN_DK:(hh + 1) * GDN_DK]
                x3 = xs.reshape(ng, halo, GDN_DK)
                acc = x3[1:ng - 1] * cw_ref[CONV_W // 2:CONV_W // 2 + 1, cols]
                for w in range(CONV_W):
                    sft = w - CONV_W // 2
                    if sft == 0:
                        continue
                    rot = pltpu.roll(x3, (-sft) % halo, axis=1)
                    if sft > 0:
                        sh = jnp.where(sub3 < halo - sft, rot[1:ng - 1], rot[2:ng])
                    else:
                        sh = jnp.where(sub3 >= -sft, rot[1:ng - 1], rot[0:ng - 2])
                    acc = acc + sh * cw_ref[w:w + 1, cols]
                acc = acc.reshape(PREP_ROWS, GDN_DK)
                half = 0.5 * acc
                y = half + half * jnp.tanh(half)
                ss = jnp.sum(y * y, axis=-1, keepdims=True)
                scale = jnp.where(is_qk, lax.rsqrt(ss + EPS) * qs, 1.0)
                o_ref[r0:r0 + PREP_ROWS, cols] = (y * scale).astype(o_ref.dtype)

    for s in range(nslab):
        project(0, s)
    for p in range(1, npieces):
        for s in range(nslab):
            project(p, s)
            conv(p - 1, s)
    for s in range(nslab):
        conv(npieces - 1, s)


def _gdn_qkv(x, nw, w, conv_w, *, tm, tn):
    t, d = x.shape
    ncols = conv_w.shape[1]
    hb = tm // V7X_SUBLANES
    nt = t // tm
    nq = GDN_QK_HEADS * GDN_DK // tn
    return pl.pallas_call(
        functools.partial(_gdn_qkv_kernel, tm=tm, nq=nq, nqk=2 * nq),
        grid=(nt, ncols // tn),
        in_specs=[
            pl.BlockSpec((V7X_SUBLANES, d), lambda i, j: (jnp.maximum(i * hb - 1, 0), 0)),
            pl.BlockSpec((tm, d), lambda i, j: (i, 0)),
            pl.BlockSpec((V7X_SUBLANES, d), lambda i, j: (jnp.minimum((i + 1) * hb, nt * hb - 1), 0)),
            pl.BlockSpec((1, d), lambda i, j: (0, 0)),
            pl.BlockSpec((d, tn), lambda i, j: (0, j)),
            pl.BlockSpec((CONV_W, tn), lambda i, j: (0, j)),
        ],
        out_specs=pl.BlockSpec((tm, tn), lambda i, j: (i, j)),
        out_shape=jax.ShapeDtypeStruct((t, ncols), BF16),
        scratch_shapes=[pltpu.VMEM((tm + 2 * V7X_SUBLANES, d), BF16)]
        + [pltpu.VMEM((tm + 2 * V7X_SUBLANES, V7X_MXU_COLS), F32) for _ in range(tn // V7X_MXU_COLS)],
        compiler_params=_cparams("parallel", "arbitrary"),
        name="gdn_qkv",
    )(x, x, x, nw.reshape(1, d), w, conv_w)


def _gdn_consts(c):
    idx = np.arange(c)
    i, m = idx[:, None], idx[None, :]
    u_f = (i <= m)
    uv_f = np.concatenate([u_f, ~u_f], axis=1).astype(np.float32)
    u_b = (i >= m)
    uv_b = np.concatenate([u_b, ~u_b], axis=1).astype(np.float32)
    uv = np.stack([uv_b, uv_f])

    def pair(b):
        return (i // (2 * b) == m // (2 * b)) & (i // b != m // b)

    lower, strict = (m <= i), (m < i)
    blk = (i // 16 == m // 16)
    levels = []
    b = 16
    while 2 * b <= c:
        levels.append(pair(b))
        b *= 2
    fwd = [lower, strict, blk & strict] + [p & strict for p in levels]
    mask_f = np.stack([x.astype(np.float32) for x in fwd])
    mask_b = np.stack([x.astype(np.float32)[::-1, ::-1] for x in fwd])
    mask = np.stack([mask_b, mask_f])
    return uv, mask, np.eye(c, dtype=np.float32), len(levels)


def _gdn_kernel(q_ref, k_ref, v_ref, z_ref, gt_ref, alog_ref, dtb_ref, uv_ref, mask_ref,
                eye_ref, onorm_ref, o_ref, s_ref, obwd_ref, mq_ref, bo_ref, et_ref,
                *, c, nci, tb, nb, nlev):
    hq = pl.program_id(0)
    d = pl.program_id(1)
    b = pl.program_id(2)

    @pl.when((hq == 0) & (d == 0) & (b == 0))
    def _():
        obwd_ref[...] = jnp.zeros_like(obwd_ref)
        s_ref[...] = jnp.zeros_like(s_ref)
        mq_ref[...] = jnp.zeros_like(mq_ref)
        bo_ref[...] = jnp.zeros_like(bo_ref)
        et_ref[...] = jnp.zeros_like(et_ref)

    row8 = lax.broadcasted_iota(jnp.int32, (V7X_SUBLANES, c), 0)

    def prepare(slot, chunk_ids):
        chunks = range(len(chunk_ids))
        units = [(ci, hh) for ci in chunks for hh in range(2)]
        rows = [slice(ca * c, (ca + 1) * c) for ca in chunk_ids]
        gts = [gt_ref[ca] for ca in chunk_ids]
        beta8 = [_sigmoid(gt) for gt in gts]
        g8 = [-jnp.exp(alog_ref[...]) * _softplus(gt + dtb_ref[...]) for gt in gts]
        gp = [jnp.concatenate(_split3(g), axis=0) for g in g8]
        qb = [q_ref[r, :] for r in rows]
        kb = [k_ref[r, :] for r in rows]
        qf = [q.astype(F32) for q in qb]
        kf = [k.astype(F32) for k in kb]
        cl = [_dot(g, uv_ref[...]) for g in gp]
        kk = [_dot_nt(k, k) for k in kb]
        qk = [_dot_nt(q, k) for q, k in zip(qb, kb)]
        kt = [k.T for k in kf]
        cl = [x[0:8] + x[8:16] + x[16:24] for x in cl]
        cum8 = [x[:, :c] for x in cl]
        rest8 = [x[:, c:] for x in cl]
        cols = [jnp.concatenate([jnp.where(row8 < 2, bt, cm), jnp.zeros((c - V7X_SUBLANES, c), F32)],
                                axis=0).T for bt, cm in zip(beta8, cum8)]
        for ci in chunks:
            et_ref[slot, chunk_ids[ci]] = jnp.exp(cum8[ci] + rest8[ci])
        yield

        beta_c = [cols[ci][:, hh:hh + 1] for ci, hh in units]
        cum_c = [cols[ci][:, 2 + hh:3 + hh] for ci, hh in units]
        dec = [jnp.exp(jnp.minimum(cc - cum8[ci][2 + hh:3 + hh, :], 0.0))
               for cc, (ci, hh) in zip(cum_c, units)]
        lm = [kk[ci] * dc * bc * mask_ref[1] for dc, bc, (ci, hh) in zip(dec, beta_c, units)]
        at = [(qk[ci] * dc * mask_ref[0]).astype(BF16) for dc, (ci, hh) in zip(dec, units)]

        n1 = [-(l * mask_ref[2]) for l in lm]
        x = [eye_ref[...] + n for n in n1]
        p = [n.astype(BF16) for n in n1]
        yield
        for _ in range(3):
            p = [_dot(pp, pp).astype(BF16) for pp in p]
            x = [xx + _dot(xx.astype(BF16), pp) for xx, pp in zip(x, p)]
            yield
        for lv in range(nlev):
            xb = [xx.astype(BF16) for xx in x]
            t1 = [_dot((l * mask_ref[3 + lv]).astype(BF16), bb).astype(BF16) for l, bb in zip(lm, xb)]
            x = [xx - _dot(bb, tt) for xx, bb, tt in zip(x, xb, t1)]
            yield

        e_c = [jnp.exp(cc) for cc in cum_c]
        rhs = [jnp.concatenate([v_ref[rows[ci], hh * GDN_DV:(hh + 1) * GDN_DV].astype(F32) * bc,
                                kf[ci] * (bc * ec)], axis=1).astype(BF16)
               for bc, ec, (ci, hh) in zip(beta_c, e_c, units)]
        uw = [_dot(xx.astype(BF16), r).astype(BF16) for xx, r in zip(x, rhs)]
        yield
        kdt = [(kt[ci] * jnp.exp(rest8[ci][2 + hh:3 + hh, :])).astype(BF16) for ci, hh in units]
        pr = [_dot(jnp.concatenate([kd, a], axis=0), w) for kd, a, w in zip(kdt, at, uw)]
        for n, (ci, hh) in enumerate(units):
            m = -pr[n][:GDN_DK, GDN_DV:]
            qm = qf[ci] * e_c[n] - pr[n][GDN_DK:, GDN_DV:]
            mq_ref[slot, chunk_ids[ci], hh] = jnp.concatenate([m, qm], axis=0).astype(BF16)
            bo_ref[slot, chunk_ids[ci], hh] = pr[n][:, :GDN_DV]

    forward = d == 1

    def walk_step(i, slot, wb):
        ci = jnp.where(forward, i, nci - 1 - i)
        r0 = pl.multiple_of(ci * c, c)
        rows = pl.ds(r0, c)
        orow = pl.ds(pl.multiple_of(wb * tb + r0, c), c)
        et = et_ref[slot, ci]
        for hh in range(2):
            s = s_ref[hh]
            r = _dot(mq_ref[slot, ci, hh], s.astype(BF16)) + bo_ref[slot, ci, hh]
            s_ref[hh] = s * et[2 + hh:3 + hh, :] + r[:GDN_DK]
            o = r[GDN_DK:]
            ocol = slice(hh * GDN_DV, (hh + 1) * GDN_DV)
            parked = obwd_ref[orow, ocol]
            ot = o + jnp.where(forward, parked, 0.0)
            obwd_ref[orow, ocol] = jnp.where(forward, parked, o)
            y = ot * _rms_scale(ot) * onorm_ref[...]
            z = z_ref[rows, ocol]
            o_ref[rows, ocol] = (y * (z * _sigmoid(z))).astype(o_ref.dtype)

    def time_block(step_b):
        return jnp.where(forward, step_b, nb - 1 - step_b)

    @pl.when(b < nb)
    def _():
        wb = time_block(jnp.maximum(b - 1, 0))
        ngrp = -(-nci // UNIT_CHUNKS)
        per = nci // ngrp
        for g in range(ngrp):
            stages = prepare(b % 2, list(range(g * per, (g + 1) * per)))
            for i in range(g * per, (g + 1) * per):
                next(stages)
                walk_step(i, (b + 1) % 2, wb)
            for _ in stages:
                pass

    @pl.when(b == 0)
    def _():
        s_ref[...] = jnp.zeros_like(s_ref)

    @pl.when(b == nb)
    def _():
        for i in range(nci):
            walk_step(i, (nb - 1) % 2, time_block(nb - 1))


def _gdn_scan(qkv, proj, gt, alog, dtb, onorm, *, t, tb):
    c = SCAN_CHUNK
    nb = t // tb
    nci = tb // c
    uv, mask, eye, nlev = _gdn_consts(c)
    hw = 2 * GDN_DV
    voff = 2 * GDN_QK_HEADS * GDN_DK // hw

    def tblk(d, b):
        return d * b + (1 - d) * (nb - 1 - b)

    def prep_blk(d, b):
        return tblk(d, jnp.minimum(b, nb - 1))

    def walk_blk(d, b):
        return tblk(d, jnp.maximum(b - 1, 0))

    return pl.pallas_call(
        functools.partial(_gdn_kernel, c=c, nci=nci, tb=tb, nb=nb, nlev=nlev),
        grid=(GDN_QK_HEADS, 2, nb + 1),
        in_specs=[
            pl.BlockSpec((tb, GDN_DK), lambda h, d, b: (prep_blk(d, b), h)),
            pl.BlockSpec((tb, GDN_DK), lambda h, d, b: (prep_blk(d, b), GDN_QK_HEADS + h)),
            pl.BlockSpec((tb, hw), lambda h, d, b: (prep_blk(d, b), voff + h)),
            pl.BlockSpec((tb, hw), lambda h, d, b: (walk_blk(d, b), h)),
            pl.BlockSpec((None, None, nci, V7X_SUBLANES, c), lambda h, d, b: (d, h, prep_blk(d, b), 0, 0)),
            pl.BlockSpec((None, None, V7X_SUBLANES, c), lambda h, d, b: (d, h, 0, 0)),
            pl.BlockSpec((None, None, V7X_SUBLANES, c), lambda h, d, b: (d, h, 0, 0)),
            pl.BlockSpec((None, c, 2 * c), lambda h, d, b: (d, 0, 0)),
            pl.BlockSpec((None, mask.shape[1], c, c), lambda h, d, b: (d, 0, 0, 0)),
            pl.BlockSpec((c, c), lambda h, d, b: (0, 0)),
            pl.BlockSpec((1, GDN_DV), lambda h, d, b: (0, 0)),
        ],
        out_specs=pl.BlockSpec((tb, hw), lambda h, d, b: (d * jnp.maximum(b - 1, 0), h)),
        out_shape=jax.ShapeDtypeStruct((t, 2 * GDN_QK_HEADS * GDN_DV), BF16),
        scratch_shapes=[pltpu.VMEM((2, GDN_DK, GDN_DV), F32),
                        pltpu.VMEM((t, hw), F32),
                        pltpu.VMEM((2, nci, 2, GDN_DK + c, GDN_DK), BF16),
                        pltpu.VMEM((2, nci, 2, GDN_DK + c, GDN_DV), F32),
                        pltpu.VMEM((2, nci, V7X_SUBLANES, c), F32)],
        compiler_params=_cparams("arbitrary", "arbitrary", "arbitrary"),
        name="gdn_scan",
    )(qkv, qkv, qkv, proj, gt, alog, dtb,
      jnp.asarray(uv, BF16), jnp.asarray(mask, F32), jnp.asarray(eye, F32),
      onorm.reshape(1, GDN_DV))


def _gdn_gate_layout(ba, a_log, dt_bias, *, t):
    c = SCAN_CHUNK
    g = ba.reshape(t, 2, 2, GDN_QK_HEADS, 2)
    g = g.transpose(2, 3, 1, 4, 0).reshape(2, GDN_QK_HEADS, 4, t)
    g = jnp.concatenate([g, jnp.zeros_like(g)], axis=2)
    g = g.reshape(2, GDN_QK_HEADS, V7X_SUBLANES, t // c, c).transpose(0, 1, 3, 2, 4)
    g = jnp.stack([g[1], g[0]])

    def rows(pv):
        pv = pv.reshape(2, GDN_QK_HEADS, 2)
        z = jnp.zeros_like(pv)
        r = jnp.concatenate([z, pv, z, z], axis=2)
        r = jnp.broadcast_to(r[..., None], (2, GDN_QK_HEADS, V7X_SUBLANES, c))
        return jnp.stack([r[1], r[0]])

    return g, rows(a_log.astype(F32)), rows(dt_bias.astype(F32))


def _pad_cols(w, n):
    return jnp.concatenate([w, jnp.zeros((w.shape[0], n - w.shape[1]), w.dtype)], axis=1)


def _proj_tile(n):
    best = None
    for tn in range(PROJ_TN_MAX, 3 * V7X_MXU_COLS, -V7X_MXU_COLS):
        n_pad = -(-n // tn) * tn
        if best is None or n_pad < best[1]:
            best = (tn, n_pad)
    return best


def kernel(x, p, mixer_norm, gla_w_in, gla_w_gate_up, gla_b_gate, gla_out_norm, gla_w_out,
           gdn_w_in, gdn_conv, gdn_a_log, gdn_dt_bias, gdn_out_norm, gdn_w_out,
           ffn_norm, ffn_w_in, ffn_w_out, ple_norm, ple_w_gate, ple_w_proj, final_norm):
    _, t, dm = x.shape
    depth = mixer_norm.shape[0]
    h = x.reshape(t, dm)
    tm = min(t, 1024)
    tm_ple = min(t, 512)
    tn_out = 1024
    tf = 512
    tn_qkv = 1024
    tb_gdn = min(t, 2048)
    for i in range(depth):
        j = i // 2
        if i % 2 == 0:
            tn, n_pad = _proj_tile(gla_w_in.shape[2])
            w_in = _pad_cols(gla_w_in[j].astype(BF16), n_pad)
            proj = _norm_matmul(h, mixer_norm[i], w_in, tm=tm, tn=tn)
            wgu, bg = _gla_gate_weights(gla_w_gate_up[j], gla_b_gate[j])
            o = _gla_scan(proj, wgu, bg, gla_out_norm[j], t=t)
            h = _matmul_res(o, gla_w_out[j].astype(BF16), h, tm=tm, tn=tn_out)
        else:
            nconv = gdn_conv.shape[2]
            vd = 2 * GDN_QK_HEADS * GDN_DV
            w_bf = gdn_w_in[j].astype(BF16)
            qkv = _gdn_qkv(h, mixer_norm[i], w_bf[:, :nconv], gdn_conv[j], tm=tm, tn=tn_qkv)
            tn, n_pad = _proj_tile(gdn_w_in.shape[2] - nconv)
            zba = _norm_matmul(h, mixer_norm[i], _pad_cols(w_bf[:, nconv:], n_pad), tm=tm, tn=tn)
            gt, alog, dtb = _gdn_gate_layout(zba[:, vd:vd + 4 * 2 * GDN_QK_HEADS], gdn_a_log[j], gdn_dt_bias[j], t=t)
            o = _gdn_scan(qkv, zba, gt, alog, dtb, gdn_out_norm[j], t=t, tb=tb_gdn)
            h = _matmul_res(o, gdn_w_out[j].astype(BF16), h, tm=tm, tn=tn_out)
        h = _ffn(h, ffn_norm[i], ffn_w_in[i].astype(BF16), ffn_w_out[i].astype(BF16), tm=tm, tf=tf)
        h = _ple(h, ple_norm[i], ple_w_gate[i].astype(BF16), p[i].reshape(t, -1),
                 ple_w_proj[i].astype(BF16), final_norm, tm=tm_ple, final=(i == depth - 1))
    return h.reshape(x.shape)
```

```python
import functools

import numpy as np
import jax
import jax.numpy as jnp
from jax import lax
from jax.experimental import pallas as pl
from jax.experimental.pallas import tpu as pltpu

F32 = jnp.float32
BF16 = jnp.bfloat16
EPS = 1e-6

V7X_LANES = 128
V7X_SUBLANES = 8
V7X_VMEM_LIMIT_BYTES = 56 * 1024 * 1024

GLA_HEADS = 4
GLA_DK = 256
GLA_DV = 512
GLA_RANK = 16
GLA_GATE_NORM = 16.0
GDN_QK_HEADS = 16
GDN_DK = 128
GDN_DV = 128
CONV_W = 5
SCAN_CHUNK = 128
LOG2_E = 1.4426950408889634
UNIT_CHUNKS = 8
SLAB_ROWS = 256
PREP_ROWS = 256
V7X_MXU_COLS = 256
PROJ_TN_MAX = 1792


def _cparams(*sem):
    return pltpu.CompilerParams(dimension_semantics=sem,
                                vmem_limit_bytes=V7X_VMEM_LIMIT_BYTES)


def _dot(a, b):
    return jnp.dot(a, b, preferred_element_type=F32)


def _dot_nt(a, b):
    return lax.dot_general(a, b, (((1,), (1,)), ((), ())), preferred_element_type=F32)


def _dot_tn(a, b):
    return lax.dot_general(a, b, (((0,), (0,)), ((), ())), preferred_element_type=F32)


def _rms_scale(x):
    return lax.rsqrt(jnp.mean(x * x, axis=-1, keepdims=True) + EPS)


def _sigmoid(x):
    return 1.0 / (1.0 + jnp.exp(-x))


def _softplus(x):
    return jnp.maximum(x, 0.0) + jnp.log(1.0 + jnp.exp(-jnp.abs(x)))


def _split3(x):
    p1 = x.astype(BF16)
    r = x - p1.astype(F32)
    p2 = r.astype(BF16)
    p3 = (r - p2.astype(F32)).astype(BF16)
    return p1, p2, p3


def _norm_matmul_kernel(x_ref, nw_ref, w_ref, o_ref, xn_ref):
    @pl.when(pl.program_id(1) == 0)
    def _():
        x = x_ref[...]
        xn_ref[...] = (x * _rms_scale(x) * nw_ref[...]).astype(BF16)

    o_ref[...] = _dot(xn_ref[...], w_ref[...]).astype(o_ref.dtype)


def _norm_matmul(x, nw, w, *, tm, tn):
    t, d = x.shape
    n = w.shape[1]
    return pl.pallas_call(
        _norm_matmul_kernel,
        grid=(t // tm, n // tn),
        in_specs=[pl.BlockSpec((tm, d), lambda i, j: (i, 0)),
                  pl.BlockSpec((1, d), lambda i, j: (0, 0)),
                  pl.BlockSpec((d, tn), lambda i, j: (0, j))],
        out_specs=pl.BlockSpec((tm, tn), lambda i, j: (i, j)),
        out_shape=jax.ShapeDtypeStruct((t, n), F32),
        scratch_shapes=[pltpu.VMEM((tm, d), BF16)],
        compiler_params=_cparams("parallel", "arbitrary"),
        name="norm_matmul",
    )(x, nw.reshape(1, d), w)


def _matmul_res_kernel(a_ref, w_ref, h_ref, o_ref):
    o_ref[...] = h_ref[...] + _dot(a_ref[...], w_ref[...])


def _matmul_res(a, w, h, *, tm, tn):
    t, k = a.shape
    n = w.shape[1]
    return pl.pallas_call(
        _matmul_res_kernel,
        grid=(t // tm, n // tn),
        in_specs=[pl.BlockSpec((tm, k), lambda i, j: (i, 0)),
                  pl.BlockSpec((k, tn), lambda i, j: (0, j)),
                  pl.BlockSpec((tm, tn), lambda i, j: (i, j))],
        out_specs=pl.BlockSpec((tm, tn), lambda i, j: (i, j)),
        out_shape=jax.ShapeDtypeStruct((t, n), F32),
        compiler_params=_cparams("parallel", "arbitrary"),
        name="matmul_res",
    )(a, w, h)


def _ffn_kernel(x_ref, nw_ref, wg_ref, wu_ref, wo_ref, o_ref, xn_ref):
    @pl.when(pl.program_id(1) == 0)
    def _():
        x = x_ref[...]
        xn_ref[...] = (x * _rms_scale(x) * nw_ref[...]).astype(BF16)
        o_ref[...] = x

    xn = xn_ref[...]
    g = _dot(xn, wg_ref[...])
    u = _dot(xn, wu_ref[...])
    act = (g * _sigmoid(g) * u).astype(BF16)
    o_ref[...] += _dot(act, wo_ref[...])


def _ffn(x, nw, w_in, w_out, *, tm, tf):
    t, d = x.shape
    f = w_out.shape[0]
    nf = f // tf
    return pl.pallas_call(
        _ffn_kernel,
        grid=(t // tm, nf),
        in_specs=[pl.BlockSpec((tm, d), lambda i, j: (i, 0)),
                  pl.BlockSpec((1, d), lambda i, j: (0, 0)),
                  pl.BlockSpec((d, tf), lambda i, j: (0, j)),
                  pl.BlockSpec((d, tf), lambda i, j: (0, nf + j)),
                  pl.BlockSpec((tf, d), lambda i, j: (j, 0))],
        out_specs=pl.BlockSpec((tm, d), lambda i, j: (i, 0)),
        out_shape=jax.ShapeDtypeStruct((t, d), F32),
        scratch_shapes=[pltpu.VMEM((tm, d), BF16)],
        compiler_params=_cparams("parallel", "arbitrary"),
        name="ffn",
    )(x, nw.reshape(1, d), w_in, w_in, w_out)


def _ple_kernel(h_ref, nw_ref, wg_ref, p_ref, wp_ref, fn_ref, o_ref, *, final):
    h = h_ref[...]
    xn = (h * _rms_scale(h) * nw_ref[...]).astype(BF16)
    gate = _sigmoid(_dot(xn, wg_ref[...]))
    y = h + gate * _dot(p_ref[...].astype(BF16), wp_ref[...])
    if final:
        y = y * _rms_scale(y) * fn_ref[...]
    o_ref[...] = y


def _ple(h, nw, wg, p, wp, fn, *, tm, final):
    t, d = h.shape
    kp = p.shape[1]
    return pl.pallas_call(
        functools.partial(_ple_kernel, final=final),
        grid=(t // tm,),
        in_specs=[pl.BlockSpec((tm, d), lambda i: (i, 0)),
                  pl.BlockSpec((1, d), lambda i: (0, 0)),
                  pl.BlockSpec((d, d), lambda i: (0, 0), pipeline_mode=pl.Buffered(1)),
                  pl.BlockSpec((tm, kp), lambda i: (i, 0)),
                  pl.BlockSpec((kp, d), lambda i: (0, 0), pipeline_mode=pl.Buffered(1)),
                  pl.BlockSpec((1, d), lambda i: (0, 0))],
        out_specs=pl.BlockSpec((tm, d), lambda i: (i, 0)),
        out_shape=jax.ShapeDtypeStruct((t, d), F32),
        compiler_params=_cparams("parallel"),
        name="ple",
    )(h, nw.reshape(1, d), wg, p, wp, fn.reshape(1, d))


def _gla_levels(c):
    out, b = [], c // 2
    while b >= 1:
        out.append(b)
        b //= 2
    return out


def _gla_consts(c):
    idx = np.arange(c)
    i, m = idx[:, None], idx[None, :]
    blocks = [(m <= i)]
    masks = []
    for b in _gla_levels(c):
        mid = (idx // (2 * b)) * (2 * b) + b - 1
        md = mid[:, None]
        blocks.append(np.where(i > md, (m > md) & (m <= i), (m > i) & (m <= md)))
        masks.append((i // (2 * b) == m // (2 * b)) & (i % (2 * b) >= b) & (m % (2 * b) < b))
    masks.append(i == m)
    fwd_w = np.concatenate([bl.astype(np.float32) for bl in blocks], axis=0)
    bwd_w = np.concatenate([bl.astype(np.float32)[::-1, ::-1] for bl in blocks], axis=0)
    ones = np.ones((V7X_SUBLANES, c), np.float32)
    wcum = np.stack([np.concatenate([bwd_w, ones], 0), np.concatenate([fwd_w, ones], 0)])
    wcum = np.concatenate([wcum, wcum], axis=2)
    fwd_m = np.stack([mk.astype(np.float32) for mk in masks])
    bwd_m = np.stack([mk.astype(np.float32)[::-1, ::-1] for mk in masks])
    mask = np.stack([bwd_m, fwd_m])
    return wcum, mask


def _gla_kernel(q_ref, k_ref, v_ref, og_ref, lr_ref, wgu_ref, bg_ref, wcum_ref, mask_ref,
                onorm_ref, o_ref, st_ref, obwd_ref, qd_ref, oa_ref, dl_ref, et_ref,
                *, c, nci, tb):
    d = pl.program_id(1)
    b = pl.program_id(2)
    nb = pl.num_programs(2)
    nlev = len(_gla_levels(c))

    @pl.when(b == 0)
    def _():
        st_ref[...] = jnp.zeros_like(st_ref)

    def prepare():
        chunks = range(nci)
        rows = [slice(ci * c, (ci + 1) * c) for ci in chunks]
        q = [q_ref[r, :] * (GLA_DK ** -0.5) for r in rows]
        k = [k_ref[r, :] for r in rows]
        vb = [v_ref[r, :].astype(BF16) for r in rows]
        gk = [_dot(lr_ref[r, :].astype(BF16), wgu_ref[...]) + bg_ref[...] for r in rows]
        g = [(jnp.minimum(x, 0.0) - jnp.log(1.0 + jnp.exp(-jnp.abs(x)))) * (LOG2_E / GLA_GATE_NORM) for x in gk]
        gp = [jnp.concatenate(_split3(x)[:2], axis=0) for x in g]
        cums = [_dot(wcum_ref[...], x) for x in gp]
        cum = [x[0:c] for x in cums]
        tot = [x[(1 + nlev) * c:(1 + nlev) * c + V7X_SUBLANES] for x in cums]
        rest = [tt[0:1] - cc for tt, cc in zip(tot, cum)]
        for ci in chunks:
            et_ref[ci] = jnp.exp2(tot[ci])

        a = [jnp.sum(qq * kk, axis=-1, keepdims=True) * mask_ref[nlev] for qq, kk in zip(q, k)]
        for lv in range(nlev):
            e = [jnp.exp2(x[(1 + lv) * c:(2 + lv) * c]) for x in cums]
            s = [_dot_nt((qq * ee).astype(BF16), (kk * ee).astype(BF16)) for qq, kk, ee in zip(q, k, e)]
            a = [aa + ss * mask_ref[lv] for aa, ss in zip(a, s)]
        a = [aa.astype(BF16) for aa in a]

        kd = [(kk * jnp.exp2(rr)).astype(BF16) for kk, rr in zip(k, rest)]
        for ci in chunks:
            qd_ref[ci] = (q[ci] * jnp.exp2(cum[ci])).astype(BF16)
            oa_ref[ci] = _dot(a[ci], vb[ci])
            dl_ref[ci] = _dot_tn(vb[ci], kd[ci])

    def walk(order, blk, forward):
        for ci in order:
            rows = slice(ci * c, (ci + 1) * c)
            st = st_ref[...]
            o = _dot_nt(qd_ref[ci], st.astype(BF16)) + oa_ref[ci]
            st_ref[...] = st * et_ref[ci][0:1] + dl_ref[ci]
            orow = pl.ds(pl.multiple_of(blk * tb + ci * c, c), c)
            if forward:
                ot = o + obwd_ref[orow, :]
                y = ot * _rms_scale(ot) * onorm_ref[...]
                og = og_ref[rows, :]
                o_ref[rows, :] = (y * (og * _sigmoid(og))).astype(o_ref.dtype)
            else:
                obwd_ref[orow, :] = o

    prepare()

    @pl.when(d == 0)
    def _():
        walk(range(nci - 1, -1, -1), nb - 1 - b, False)

    @pl.when(d == 1)
    def _():
        walk(range(nci), b, True)


def _gla_scan(proj, wgu, bg, onorm, *, t):
    c = SCAN_CHUNK
    tb = min(t, 512)
    nb = t // tb
    nci = tb // c
    wcum, mask = _gla_consts(c)
    nlev = len(_gla_levels(c))
    qoff = 0
    koff = GLA_HEADS * GLA_DK // GLA_DK
    voff = 2 * GLA_HEADS * GLA_DK // GLA_DV
    goff = voff + GLA_HEADS
    lroff = (2 * GLA_HEADS * GLA_DK + 2 * GLA_HEADS * GLA_DV) // V7X_LANES

    def tblk(d, b):
        return d * b + (1 - d) * (nb - 1 - b)

    return pl.pallas_call(
        functools.partial(_gla_kernel, c=c, nci=nci, tb=tb),
        grid=(GLA_HEADS, 2, nb),
        in_specs=[
            pl.BlockSpec((tb, GLA_DK), lambda h, d, b: (tblk(d, b), qoff + h)),
            pl.BlockSpec((tb, GLA_DK), lambda h, d, b: (tblk(d, b), koff + h)),
            pl.BlockSpec((tb, GLA_DV), lambda h, d, b: (tblk(d, b), voff + h)),
            pl.BlockSpec((tb, GLA_DV), lambda h, d, b: (tblk(d, b), goff + h)),
            pl.BlockSpec((tb, V7X_LANES), lambda h, d, b: (tblk(d, b), lroff)),
            pl.BlockSpec((None, None, V7X_LANES, GLA_DK), lambda h, d, b: (d, h, 0, 0)),
            pl.BlockSpec((None, None, 1, GLA_DK), lambda h, d, b: (d, h, 0, 0)),
            pl.BlockSpec((None, wcum.shape[1], 2 * c), lambda h, d, b: (d, 0, 0)),
            pl.BlockSpec((None, nlev + 1, c, c), lambda h, d, b: (d, 0, 0, 0)),
            pl.BlockSpec((1, GLA_DV), lambda h, d, b: (0, 0)),
        ],
        out_specs=pl.BlockSpec((tb, GLA_DV), lambda h, d, b: (d * b, h)),
        out_shape=jax.ShapeDtypeStruct((t, GLA_HEADS * GLA_DV), BF16),
        scratch_shapes=[pltpu.VMEM((GLA_DV, GLA_DK), F32),
                        pltpu.VMEM((t, GLA_DV), F32),
                        pltpu.VMEM((nci, c, GLA_DK), BF16),
                        pltpu.VMEM((nci, c, GLA_DV), F32),
                        pltpu.VMEM((nci, GLA_DV, GLA_DK), F32),
                        pltpu.VMEM((nci, V7X_SUBLANES, GLA_DK), F32)],
        compiler_params=_cparams("arbitrary", "arbitrary", "arbitrary"),
        name="gla_scan",
    )(proj, proj, proj, proj, proj, wgu, bg,
      jnp.asarray(wcum, BF16), jnp.asarray(mask, F32),
      onorm.reshape(1, GLA_DV))


def _gla_gate_weights(w_gate_up, b_gate):
    w = w_gate_up.reshape(2, GLA_RANK, GLA_HEADS, GLA_DK).transpose(0, 2, 1, 3)
    z = jnp.zeros((GLA_HEADS, V7X_LANES, GLA_DK), F32)
    fwd = z.at[:, 0:GLA_RANK].set(w[0])
    bwd = z.at[:, GLA_RANK:2 * GLA_RANK].set(w[1])
    wgu = jnp.stack([bwd, fwd]).astype(BF16)
    bg = jnp.stack([b_gate[1], b_gate[0]]).reshape(2, GLA_HEADS, 1, GLA_DK)
    return wgu, bg


def _gdn_qkv_kernel(prev_ref, cur_ref, next_ref, nw_ref, w_ref, cw_ref, o_ref, xn_ref, *r_refs, tm, nq, nqk):
    i = pl.program_id(0)
    j = pl.program_id(1)
    halo = V7X_SUBLANES

    @pl.when(j == 0)
    def _():
        def normed(x):
            return (x * _rms_scale(x) * nw_ref[...]).astype(BF16)

        zero = jnp.zeros((halo, xn_ref.shape[1]), BF16)
        xn_ref[0:halo, :] = jnp.where(i > 0, normed(prev_ref[...]), zero)
        xn_ref[halo:halo + tm, :] = normed(cur_ref[...])
        xn_ref[halo + tm:2 * halo + tm, :] = jnp.where(i < pl.num_programs(0) - 1, normed(next_ref[...]), zero)

    is_qk = j < nqk
    qs = jnp.where(j < nq, GDN_DK ** -0.5, 1.0)
    n = PREP_ROWS + 2 * halo
    pw = V7X_MXU_COLS
    npieces = w_ref.shape[1] // pw

    nslab = tm // SLAB_ROWS
    edges = [0] + [(s + 1) * SLAB_ROWS + 2 * halo for s in range(nslab)]

    def project(p, s):
        rows = slice(edges[s], edges[s + 1])
        r_refs[p][rows, :] = _dot(xn_ref[rows, :], w_ref[:, p * pw:(p + 1) * pw])

    ng = n // halo
    sub3 = lax.broadcasted_iota(jnp.int32, (ng - 2, halo, GDN_DK), 1)

    def conv(p, s):
        for r0 in range(s * SLAB_ROWS, (s + 1) * SLAB_ROWS, PREP_ROWS):
            for hh in range(pw // GDN_DK):
                cols = slice(p * pw + hh * GDN_DK, p * pw + (hh + 1) * GDN_DK)
                xs = r_refs[p][r0:r0 + n, hh * GDN_DK:(hh + 1) * GDN_DK]
                x3 = xs.reshape(ng, halo, GDN_DK)
                acc = x3[1:ng - 1] * cw_ref[CONV_W // 2:CONV_W // 2 + 1, cols]
                for w in range(CONV_W):
                    sft = w - CONV_W // 2
                    if sft == 0:
                        continue
                    rot = pltpu.roll(x3, (-sft) % halo, axis=1)
                    if sft > 0:
                        sh = jnp.where(sub3 < halo - sft, rot[1:ng - 1], rot[2:ng])
                    else:
                        sh = jnp.where(sub3 >= -sft, rot[1:ng - 1], rot[0:ng - 2])
                    acc = acc + sh * cw_ref[w:w + 1, cols]
                acc = acc.reshape(PREP_ROWS, GDN_DK)
                half = 0.5 * acc
                y = half + half * jnp.tanh(half)
                ss = jnp.sum(y * y, axis=-1, keepdims=True)
                scale = jnp.where(is_qk, lax.rsqrt(ss + EPS) * qs, 1.0)
                o_ref[r0:r0 + PREP_ROWS, cols] = (y * scale).astype(o_ref.dtype)

    for s in range(nslab):
        project(0, s)
    for p in range(1, npieces):
        for s in range(nslab):
            project(p, s)
            conv(p - 1, s)
    for s in range(nslab):
        conv(npieces - 1, s)


def _gdn_qkv(x, nw, w, conv_w, *, tm, tn):
    t, d = x.shape
    ncols = conv_w.shape[1]
    hb = tm // V7X_SUBLANES
    nt = t // tm
    nq = GDN_QK_HEADS * GDN_DK // tn
    return pl.pallas_call(
        functools.partial(_gdn_qkv_kernel, tm=tm, nq=nq, nqk=2 * nq),
        grid=(nt, ncols // tn),
        in_specs=[
            pl.BlockSpec((V7X_SUBLANES, d), lambda i, j: (jnp.maximum(i * hb - 1, 0), 0)),
            pl.BlockSpec((tm, d), lambda i, j: (i, 0)),
            pl.BlockSpec((V7X_SUBLANES, d), lambda i, j: (jnp.minimum((i + 1) * hb, nt * hb - 1), 0)),
            pl.BlockSpec((1, d), lambda i, j: (0, 0)),
            pl.BlockSpec((d, tn), lambda i, j: (0, j)),
            pl.BlockSpec((CONV_W, tn), lambda i, j: (0, j)),
        ],
        out_specs=pl.BlockSpec((tm, tn), lambda i, j: (i, j)),
        out_shape=jax.ShapeDtypeStruct((t, ncols), BF16),
        scratch_shapes=[pltpu.VMEM((tm + 2 * V7X_SUBLANES, d), BF16)]
        + [pltpu.VMEM((tm + 2 * V7X_SUBLANES, V7X_MXU_COLS), F32) for _ in range(tn // V7X_MXU_COLS)],
        compiler_params=_cparams("parallel", "arbitrary"),
        name="gdn_qkv",
    )(x, x, x, nw.reshape(1, d), w, conv_w)


def _gdn_consts(c):
    idx = np.arange(c)
    i, m = idx[:, None], idx[None, :]
    u_f = (i <= m)
    uv_f = np.concatenate([u_f, ~u_f], axis=1).astype(np.float32)
    u_b = (i >= m)
    uv_b = np.concatenate([u_b, ~u_b], axis=1).astype(np.float32)
    uv = np.stack([uv_b, uv_f])

    def pair(b):
        return (i // (2 * b) == m // (2 * b)) & (i // b != m // b)

    lower, strict = (m <= i), (m < i)
    blk = (i // 16 == m // 16)
    levels = []
    b = 16
    while 2 * b <= c:
        levels.append(pair(b))
        b *= 2
    fwd = [lower, strict, blk & strict] + [p & strict for p in levels]
    mask_f = np.stack([x.astype(np.float32) for x in fwd])
    mask_b = np.stack([x.astype(np.float32)[::-1, ::-1] for x in fwd])
    mask = np.stack([mask_b, mask_f])
    return uv, mask, np.eye(c, dtype=np.float32), len(levels)


def _gdn_kernel(q_ref, k_ref, v_ref, z_ref, gt_ref, alog_ref, dtb_ref, uv_ref, mask_ref,
                eye_ref, onorm_ref, o_ref, s_ref, obwd_ref, mq_ref, bo_ref, et_ref,
                *, c, nci, tb, nb, nlev):
    hq = pl.program_id(0)
    d = pl.program_id(1)
    b = pl.program_id(2)

    @pl.when((hq == 0) & (d == 0) & (b == 0))
    def _():
        obwd_ref[...] = jnp.zeros_like(obwd_ref)
        s_ref[...] = jnp.zeros_like(s_ref)
        mq_ref[...] = jnp.zeros_like(mq_ref)
        bo_ref[...] = jnp.zeros_like(bo_ref)
        et_ref[...] = jnp.zeros_like(et_ref)

    row8 = lax.broadcasted_iota(jnp.int32, (V7X_SUBLANES, c), 0)

    def prepare(slot, chunk_ids):
        chunks = range(len(chunk_ids))
        units = [(ci, hh) for ci in chunks for hh in range(2)]
        rows = [slice(ca * c, (ca + 1) * c) for ca in chunk_ids]
        gts = [gt_ref[ca] for ca in chunk_ids]
        beta8 = [_sigmoid(gt) for gt in gts]
        g8 = [-jnp.exp(alog_ref[...]) * _softplus(gt + dtb_ref[...]) for gt in gts]
        gp = [jnp.concatenate(_split3(g), axis=0) for g in g8]
        qb = [q_ref[r, :] for r in rows]
        kb = [k_ref[r, :] for r in rows]
        qf = [q.astype(F32) for q in qb]
        kf = [k.astype(F32) for k in kb]
        cl = [_dot(g, uv_ref[...]) for g in gp]
        kk = [_dot_nt(k, k) for k in kb]
        qk = [_dot_nt(q, k) for q, k in zip(qb, kb)]
        kt = [k.T for k in kf]
        cl = [x[0:8] + x[8:16] + x[16:24] for x in cl]
        cum8 = [x[:, :c] for x in cl]
        rest8 = [x[:, c:] for x in cl]
        cols = [jnp.concatenate([jnp.where(row8 < 2, bt, cm), jnp.zeros((c - V7X_SUBLANES, c), F32)],
                                axis=0).T for bt, cm in zip(beta8, cum8)]
        for ci in chunks:
            et_ref[slot, chunk_ids[ci]] = jnp.exp(cum8[ci] + rest8[ci])
        yield

        beta_c = [cols[ci][:, hh:hh + 1] for ci, hh in units]
        cum_c = [cols[ci][:, 2 + hh:3 + hh] for ci, hh in units]
        dec = [jnp.exp(jnp.minimum(cc - cum8[ci][2 + hh:3 + hh, :], 0.0))
               for cc, (ci, hh) in zip(cum_c, units)]
        lm = [kk[ci] * dc * bc * mask_ref[1] for dc, bc, (ci, hh) in zip(dec, beta_c, units)]
        at = [(qk[ci] * dc * mask_ref[0]).astype(BF16) for dc, (ci, hh) in zip(dec, units)]

        n1 = [-(l * mask_ref[2]) for l in lm]
        x = [eye_ref[...] + n for n in n1]
        p = [n.astype(BF16) for n in n1]
        yield
        for _ in range(3):
            p = [_dot(pp, pp).astype(BF16) for pp in p]
            x = [xx + _dot(xx.astype(BF16), pp) for xx, pp in zip(x, p)]
            yield
        for lv in range(nlev):
            xb = [xx.astype(BF16) for xx in x]
            t1 = [_dot((l * mask_ref[3 + lv]).astype(BF16), bb).astype(BF16) for l, bb in zip(lm, xb)]
            x = [xx - _dot(bb, tt) for xx, bb, tt in zip(x, xb, t1)]
            yield

        e_c = [jnp.exp(cc) for cc in cum_c]
        rhs = [jnp.concatenate([v_ref[rows[ci], hh * GDN_DV:(hh + 1) * GDN_DV].astype(F32) * bc,
                                kf[ci] * (bc * ec)], axis=1).astype(BF16)
               for bc, ec, (ci, hh) in zip(beta_c, e_c, units)]
        uw = [_dot(xx.astype(BF16), r).astype(BF16) for xx, r in zip(x, rhs)]
        yield
        kdt = [(kt[ci] * jnp.exp(rest8[ci][2 + hh:3 + hh, :])).astype(BF16) for ci, hh in units]
        pr = [_dot(jnp.concatenate([kd, a], axis=0), w) for kd, a, w in zip(kdt, at, uw)]
        for n, (ci, hh) in enumerate(units):
            m = -pr[n][:GDN_DK, GDN_DV:]
            qm = qf[ci] * e_c[n] - pr[n][GDN_DK:, GDN_DV:]
            mq_ref[slot, chunk_ids[ci], hh] = jnp.concatenate([m, qm], axis=0).astype(BF16)
            bo_ref[slot, chunk_ids[ci], hh] = pr[n][:, :GDN_DV]

    forward = d == 1

    def walk_step(i, slot, wb):
        ci = jnp.where(forward, i, nci - 1 - i)
        r0 = pl.multiple_of(ci * c, c)
        rows = pl.ds(r0, c)
        orow = pl.ds(pl.multiple_of(wb * tb + r0, c), c)
        et = et_ref[slot, ci]
        for hh in range(2):
            s = s_ref[hh]
            r = _dot(mq_ref[slot, ci, hh], s.astype(BF16)) + bo_ref[slot, ci, hh]
            s_ref[hh] = s * et[2 + hh:3 + hh, :] + r[:GDN_DK]
            o = r[GDN_DK:]
            ocol = slice(hh * GDN_DV, (hh + 1) * GDN_DV)
            parked = obwd_ref[orow, ocol]
            ot = o + jnp.where(forward, parked, 0.0)
            obwd_ref[orow, ocol] = jnp.where(forward, parked, o)
            y = ot * _rms_scale(ot) * onorm_ref[...]
            z = z_ref[rows, ocol]
            o_ref[rows, ocol] = (y * (z * _sigmoid(z))).astype(o_ref.dtype)

    def time_block(step_b):
        return jnp.where(forward, step_b, nb - 1 - step_b)

    @pl.when(b < nb)
    def _():
        wb = time_block(jnp.maximum(b - 1, 0))
        ngrp = -(-nci // UNIT_CHUNKS)
        per = nci // ngrp
        for g in range(ngrp):
            stages = prepare(b % 2, list(range(g * per, (g + 1) * per)))
            for i in range(g * per, (g + 1) * per):
                next(stages)
                walk_step(i, (b + 1) % 2, wb)
            for _ in stages:
                pass

    @pl.when(b == 0)
    def _():
        s_ref[...] = jnp.zeros_like(s_ref)

    @pl.when(b == nb)
    def _():
        for i in range(nci):
            walk_step(i, (nb - 1) % 2, time_block(nb - 1))


def _gdn_scan(qkv, proj, gt, alog, dtb, onorm, *, t, tb):
    c = SCAN_CHUNK
    nb = t // tb
    nci = tb // c
    uv, mask, eye, nlev = _gdn_consts(c)
    hw = 2 * GDN_DV
    voff = 2 * GDN_QK_HEADS * GDN_DK // hw

    def tblk(d, b):
        return d * b + (1 - d) * (nb - 1 - b)

    def prep_blk(d, b):
        return tblk(d, jnp.minimum(b, nb - 1))

    def walk_blk(d, b):
        return tblk(d, jnp.maximum(b - 1, 0))

    return pl.pallas_call(
        functools.partial(_gdn_kernel, c=c, nci=nci, tb=tb, nb=nb, nlev=nlev),
        grid=(GDN_QK_HEADS, 2, nb + 1),
        in_specs=[
            pl.BlockSpec((tb, GDN_DK), lambda h, d, b: (prep_blk(d, b), h)),
            pl.BlockSpec((tb, GDN_DK), lambda h, d, b: (prep_blk(d, b), GDN_QK_HEADS + h)),
            pl.BlockSpec((tb, hw), lambda h, d, b: (prep_blk(d, b), voff + h)),
            pl.BlockSpec((tb, hw), lambda h, d, b: (walk_blk(d, b), h)),
            pl.BlockSpec((None, None, nci, V7X_SUBLANES, c), lambda h, d, b: (d, h, prep_blk(d, b), 0, 0)),
            pl.BlockSpec((None, None, V7X_SUBLANES, c), lambda h, d, b: (d, h, 0, 0)),
            pl.BlockSpec((None, None, V7X_SUBLANES, c), lambda h, d, b: (d, h, 0, 0)),
            pl.BlockSpec((None, c, 2 * c), lambda h, d, b: (d, 0, 0)),
            pl.BlockSpec((None, mask.shape[1], c, c), lambda h, d, b: (d, 0, 0, 0)),
            pl.BlockSpec((c, c), lambda h, d, b: (0, 0)),
            pl.BlockSpec((1, GDN_DV), lambda h, d, b: (0, 0)),
        ],
        out_specs=pl.BlockSpec((tb, hw), lambda h, d, b: (d * jnp.maximum(b - 1, 0), h)),
        out_shape=jax.ShapeDtypeStruct((t, 2 * GDN_QK_HEADS * GDN_DV), BF16),
        scratch_shapes=[pltpu.VMEM((2, GDN_DK, GDN_DV), F32),
                        pltpu.VMEM((t, hw), F32),
                        pltpu.VMEM((2, nci, 2, GDN_DK + c, GDN_DK), BF16),
                        pltpu.VMEM((2, nci, 2, GDN_DK + c, GDN_DV), F32),
                        pltpu.VMEM((2, nci, V7X_SUBLANES, c), F32)],
        compiler_params=_cparams("arbitrary", "arbitrary", "arbitrary"),
        name="gdn_scan",
    )(qkv, qkv, qkv, proj, gt, alog, dtb,
      jnp.asarray(uv, BF16), jnp.asarray(mask, F32), jnp.asarray(eye, F32),
      onorm.reshape(1, GDN_DV))


def _gdn_gate_layout(ba, a_log, dt_bias, *, t):
    c = SCAN_CHUNK
    g = ba.reshape(t, 2, 2, GDN_QK_HEADS, 2)
    g = g.transpose(2, 3, 1, 4, 0).reshape(2, GDN_QK_HEADS, 4, t)
    g = jnp.concatenate([g, jnp.zeros_like(g)], axis=2)
    g = g.reshape(2, GDN_QK_HEADS, V7X_SUBLANES, t // c, c).transpose(0, 1, 3, 2, 4)
    g = jnp.stack([g[1], g[0]])

    def rows(pv):
        pv = pv.reshape(2, GDN_QK_HEADS, 2)
        z = jnp.zeros_like(pv)
        r = jnp.concatenate([z, pv, z, z], axis=2)
        r = jnp.broadcast_to(r[..., None], (2, GDN_QK_HEADS, V7X_SUBLANES, c))
        return jnp.stack([r[1], r[0]])

    return g, rows(a_log.astype(F32)), rows(dt_bias.astype(F32))


def _pad_cols(w, n):
    return jnp.concatenate([w, jnp.zeros((w.shape[0], n - w.shape[1]), w.dtype)], axis=1)


def _proj_tile(n):
    best = None
    for tn in range(PROJ_TN_MAX, 3 * V7X_MXU_COLS, -V7X_MXU_COLS):
        n_pad = -(-n // tn) * tn
        if best is None or n_pad < best[1]:
            best = (tn, n_pad)
    return best


def kernel(x, p, mixer_norm, gla_w_in, gla_w_gate_up, gla_b_gate, gla_out_norm, gla_w_out,
           gdn_w_in, gdn_conv, gdn_a_log, gdn_dt_bias, gdn_out_norm, gdn_w_out,
           ffn_norm, ffn_w_in, ffn_w_out, ple_norm, ple_w_gate, ple_w_proj, final_norm):
    _, t, dm = x.shape
    depth = mixer_norm.shape[0]
    h = x.reshape(t, dm)
    tm = min(t, 1024)
    tm_ple = min(t, 512)
    tn_out = 1024
    tf = 512
    tn_qkv = 1024
    tb_gdn = min(t, 2048)
    for i in range(depth):
        j = i // 2
        if i % 2 == 0:
            tn, n_pad = _proj_tile(gla_w_in.shape[2])
            w_in = _pad_cols(gla_w_in[j].astype(BF16), n_pad)
            proj = _norm_matmul(h, mixer_norm[i], w_in, tm=tm, tn=tn)
            wgu, bg = _gla_gate_weights(gla_w_gate_up[j], gla_b_gate[j])
            o = _gla_scan(proj, wgu, bg, gla_out_norm[j], t=t)
            h = _matmul_res(o, gla_w_out[j].astype(BF16), h, tm=tm, tn=tn_out)
        else:
            nconv = gdn_conv.shape[2]
            vd = 2 * GDN_QK_HEADS * GDN_DV
            w_qkv = gdn_w_in[j, :, :nconv].astype(BF16)
            w_zba = gdn_w_in[j, :, nconv:].astype(BF16)
            qkv = _gdn_qkv(h, mixer_norm[i], w_qkv, gdn_conv[j], tm=tm, tn=tn_qkv)
            tn, n_pad = _proj_tile(gdn_w_in.shape[2] - nconv)
            zba = _norm_matmul(h, mixer_norm[i], _pad_cols(w_zba, n_pad), tm=tm, tn=tn)
            gt, alog, dtb = _gdn_gate_layout(zba[:, vd:vd + 4 * 2 * GDN_QK_HEADS], gdn_a_log[j], gdn_dt_bias[j], t=t)
            o = _gdn_scan(qkv, zba, gt, alog, dtb, gdn_out_norm[j], t=t, tb=tb_gdn)
            h = _matmul_res(o, gdn_w_out[j].astype(BF16), h, tm=tm, tn=tn_out)
        h = _ffn(h, ffn_norm[i], ffn_w_in[i].astype(BF16), ffn_w_out[i].astype(BF16), tm=tm, tf=tf)
        h = _ple(h, ple_norm[i], ple_w_gate[i].astype(BF16), p[i].reshape(t, -1),
                 ple_w_proj[i].astype(BF16), final_norm, tm=tm_ple, final=(i == depth - 1))
    return h.reshape(x.shape)
```
